```python
import math
import jax
import jax.numpy as jnp
from jax import lax
import numpy as np

D_MODEL = 1024
BATCH = 4
SEQ = 4096
DEPTH = 2
DEC_BATCH = 32
DEC_SEQ = 1
PAST_LEN = 16384
PAGE_SIZE = 128

N_HEADS_A = 8
HEAD_DIM = 64
DV_A = 2 * HEAD_DIM
N_HEADS_B = 16
ROPE_THETA = 10000.0
NORM_EPS = 1e-6
SUBLN_EPS = 1e-5
Q_BLOCK = 128
PEER_HEADS = 8
PEER_N_KEYS = 128
PEER_N_EXPERTS = PEER_N_KEYS * PEER_N_KEYS
PEER_TOPK = 16
PEER_QDIM = 256
PEER_HALF = PEER_QDIM // 2
PEER_TOKEN_BLOCK = 256
PLE_DIM = 256
N_LAYERS_A = (DEPTH + 1) // 2
N_LAYERS_B = DEPTH // 2

kernel_name = "hybrid_diffattn_stickbreak_peer_step"


def _rms_norm(x, g, eps=NORM_EPS):
    xf = x.astype(jnp.float32)
    y = xf * lax.rsqrt(jnp.mean(xf * xf, axis=-1, keepdims=True) + eps)
    return (y * g.astype(jnp.float32)).astype(x.dtype)


def _rope(x, pos):
    d = x.shape[-1]
    half = d // 2
    inv = 1.0 / (ROPE_THETA ** (jnp.arange(half, dtype=jnp.float32) * (2.0 / d)))
    ang = pos.astype(jnp.float32)[:, None] * inv[None, :]
    cos = jnp.cos(ang)[None, :, None, :]
    sin = jnp.sin(ang)[None, :, None, :]
    xf = x.astype(jnp.float32)
    x1, x2 = xf[..., :half], xf[..., half:]
    return jnp.concatenate([x1 * cos - x2 * sin, x2 * cos + x1 * sin], axis=-1).astype(x.dtype)


def _query_blocks(fn, qs, qpos):
    t = qpos.shape[0]
    qb = math.gcd(t, Q_BLOCK)
    nb = t // qb

    def split(a):
        return jnp.moveaxis(a.reshape(a.shape[0], nb, qb, *a.shape[2:]), 1, 0)

    out = lax.map(lambda blk: fn(blk[0], blk[1]), (tuple(split(q) for q in qs), qpos.reshape(nb, qb)))
    out = jnp.moveaxis(out, 0, 1)
    return out.reshape(out.shape[0], t, *out.shape[3:])


def _scores(spec, q, ks):
    return jnp.concatenate([jnp.einsum(spec, q, k, preferred_element_type=jnp.float32) for k in ks], axis=-1)


def _mix(w, vs):
    out, off = None, 0
    for v in vs:
        n = v.shape[1]
        part = jnp.einsum('bhqk,bkhe->bqhe', w[..., off:off + n].astype(v.dtype), v)
        out = part if out is None else out + part
        off += n
    return out


def _diff_qkv(h, w_qkv, pos):
    b, t, _ = h.shape
    q, k, v = jnp.split(h @ w_qkv, 3, axis=-1)
    q = _rope(q.reshape(b, t, 2 * N_HEADS_A, HEAD_DIM), pos)
    k = _rope(k.reshape(b, t, 2 * N_HEADS_A, HEAD_DIM), pos)
    v = v.reshape(b, t, N_HEADS_A, DV_A)
    return q, k, v


def _diff_lambda(lq1, lk1, lq2, lk2, lam_init):
    f = lambda a: a.astype(jnp.float32)
    return jnp.exp(jnp.sum(f(lq1) * f(lk1))) - jnp.exp(jnp.sum(f(lq2) * f(lk2))) + lam_init


def _diff_attend(q, qpos, ks, vs, kpos, lam, subln, lam_init, w_o):
    b, t = q.shape[:2]
    pair = lambda a: a.reshape(a.shape[0], a.shape[1], N_HEADS_A, 2, HEAD_DIM)
    ks = [pair(k) for k in ks]
    scale = HEAD_DIM ** -0.5

    def block(qb, pb):
        s = _scores('bqhcd,bkhcd->bchqk', qb[0] * scale, ks)
        mask = kpos[None, :] <= pb[:, None]
        a = jax.nn.softmax(jnp.where(mask, s, -jnp.inf), axis=-1)
        return _mix(a[:, 0] - lam * a[:, 1], vs)

    o = _query_blocks(block, (pair(q),), qpos)
    o = _rms_norm(o, subln, SUBLN_EPS) * (1.0 - lam_init)
    return o.reshape(b, t, -1) @ w_o


def _stick_qkv(h, w_qkv):
    b, t, _ = h.shape
    q, k, v = jnp.split(h @ w_qkv, 3, axis=-1)
    shp = (b, t, N_HEADS_B, HEAD_DIM)
    return q.reshape(shp), k.reshape(shp), v.reshape(shp)


def _stick_attend(q, qpos, ks, vs, kpos, w_o):
    b, t = q.shape[:2]
    scale = HEAD_DIM ** -0.5

    def block(qb, pb):
        z = _scores('bqhd,bkhd->bhqk', qb[0] * scale, ks)
        mask = kpos[None, :] < pb[:, None]
        log_keep = jnp.where(mask, jax.nn.log_sigmoid(-z), 0.0)
        tail = lax.cumsum(log_keep, axis=3, reverse=True) - log_keep
        w = jnp.where(mask, jnp.exp(jax.nn.log_sigmoid(z) + tail), 0.0)
        return _mix(w, vs)

    o = _query_blocks(block, (q,), qpos)
    return o.reshape(b, t, -1) @ w_o


def _peer(h, w_q, sub_keys, u, v):
    b, t, d = h.shape
    n = b * t
    x = h.reshape(n, d)
    q = (x @ w_q).reshape(n, PEER_HEADS, 2, PEER_HALF)
    s = jnp.einsum('nhcd,hckd->nhck', q, sub_keys, preferred_element_type=jnp.float32)
    s1, i1 = lax.top_k(s[:, :, 0], PEER_TOPK)
    s2, i2 = lax.top_k(s[:, :, 1], PEER_TOPK)
    cand_s = (s1[..., :, None] + s2[..., None, :]).reshape(n, PEER_HEADS, PEER_TOPK * PEER_TOPK)
    cand_i = (i1[..., :, None] * PEER_N_KEYS + i2[..., None, :]).reshape(n, PEER_HEADS, PEER_TOPK * PEER_TOPK)
    top_s, sel = lax.top_k(cand_s, PEER_TOPK)
    idx = jnp.take_along_axis(cand_i, sel, axis=-1)
    gate = jax.nn.softmax(top_s, axis=-1)
    tb = math.gcd(n, PEER_TOKEN_BLOCK)
    nb = n // tb

    def block(args):
        xb, ib, gb = args
        act = jax.nn.gelu(jnp.einsum('nd,nhkd->nhk', xb, u[ib], preferred_element_type=jnp.float32), approximate=False)
        return jnp.einsum('nhk,nhkd->nd', (gb * act).astype(v.dtype), v[ib])

    y = lax.map(block, (x.reshape(nb, tb, d),
                        idx.reshape(nb, tb, PEER_HEADS, PEER_TOPK),
                        gate.reshape(nb, tb, PEER_HEADS, PEER_TOPK)))
    return y.reshape(b, t, d)


def _ple(r, p, g, w_gate, w_proj):
    return r + jax.nn.sigmoid(_rms_norm(r, g) @ w_gate) * (p @ w_proj)


def setup_inputs(seed: int = 0) -> dict:
    key = jax.random.key(seed)
    k = jax.random.split(key, 26)
    n_pages = PAST_LEN // PAGE_SIZE
    n_used = DEC_BATCH * n_pages
    n_pool = n_used + max(1, n_used // 4)
    nrm = lambda kk, shape, scale: jax.random.normal(kk, shape, jnp.float32) * scale
    lam = nrm(k[15], (4, N_LAYERS_A, HEAD_DIM), 0.1)
    return {
        'x_prompt': nrm(k[0], (BATCH, SEQ, D_MODEL), 1.0),
        'x_sample': nrm(k[1], (DEC_BATCH, DEC_SEQ, D_MODEL), 1.0),
        'p_prompt': nrm(k[2], (DEPTH, BATCH, SEQ, PLE_DIM), 1.0),
        'p_sample': nrm(k[3], (DEPTH, DEC_BATCH, DEC_SEQ, PLE_DIM), 1.0),
        'cache_k_diff': nrm(k[4], (N_LAYERS_A, n_pool, PAGE_SIZE, 2 * N_HEADS_A, HEAD_DIM), 1.0),
        'cache_v_diff': nrm(k[5], (N_LAYERS_A, n_pool, PAGE_SIZE, N_HEADS_A, DV_A), 1.0),
        'cache_k_stick': nrm(k[6], (N_LAYERS_B, n_pool, PAGE_SIZE, N_HEADS_B, HEAD_DIM), 1.0),
        'cache_v_stick': nrm(k[7], (N_LAYERS_B, n_pool, PAGE_SIZE, N_HEADS_B, HEAD_DIM), 1.0),
        'page_table': jax.random.permutation(k[8], n_pool)[:n_used].reshape(DEC_BATCH, n_pages).astype(jnp.int32),
        'norm_mix': 1.0 + nrm(k[9], (DEPTH, D_MODEL), 0.01),
        'norm_ffn': 1.0 + nrm(k[10], (DEPTH, D_MODEL), 0.01),
        'norm_ple': 1.0 + nrm(k[11], (DEPTH, D_MODEL), 0.01),
        'norm_final': 1.0 + nrm(k[12], (D_MODEL,), 0.01),
        'w_qkv_diff': nrm(k[13], (N_LAYERS_A, D_MODEL, 3 * D_MODEL), D_MODEL ** -0.5),
        'w_o_diff': nrm(k[14], (N_LAYERS_A, D_MODEL, D_MODEL), D_MODEL ** -0.5),
        'lambda_q1': lam[0],
        'lambda_k1': lam[1],
        'lambda_q2': lam[2],
        'lambda_k2': lam[3],
        'subln_diff': 1.0 + nrm(k[16], (N_LAYERS_A, DV_A), 0.01),
        'w_qkv_stick': nrm(k[17], (N_LAYERS_B, D_MODEL, 3 * D_MODEL), D_MODEL ** -0.5),
        'w_o_stick': nrm(k[18], (N_LAYERS_B, D_MODEL, D_MODEL), D_MODEL ** -0.5),
        'peer_w_q': nrm(k[19], (DEPTH, D_MODEL, PEER_HEADS * PEER_QDIM), D_MODEL ** -0.5),
        'peer_sub_keys': nrm(k[20], (DEPTH, PEER_HEADS, 2, PEER_N_KEYS, PEER_HALF), PEER_HALF ** -0.5),
        'peer_u': nrm(k[21], (DEPTH, PEER_N_EXPERTS, D_MODEL), D_MODEL ** -0.5),
        'peer_v': nrm(k[22], (DEPTH, PEER_N_EXPERTS, D_MODEL), PEER_HEADS ** -0.5),
        'ple_w_gate': nrm(k[23], (DEPTH, D_MODEL, D_MODEL), D_MODEL ** -0.5),
        'ple_w_proj': nrm(k[24], (DEPTH, PLE_DIM, D_MODEL), PLE_DIM ** -0.5),
    }


def reference(x_prompt, x_sample, p_prompt, p_sample, cache_k_diff, cache_v_diff, cache_k_stick, cache_v_stick,
              page_table, norm_mix, norm_ffn, norm_ple, norm_final, w_qkv_diff, w_o_diff,
              lambda_q1, lambda_k1, lambda_q2, lambda_k2, subln_diff, w_qkv_stick, w_o_stick,
              peer_w_q, peer_sub_keys, peer_u, peer_v, ple_w_gate, ple_w_proj):
    seq = x_prompt.shape[1]
    dec_batch, dec_seq = x_sample.shape[0], x_sample.shape[1]
    n_pages = page_table.shape[1]
    past_len = n_pages * PAGE_SIZE
    pos_p = jnp.arange(seq, dtype=jnp.int32)
    pos_s = past_len + jnp.arange(dec_seq, dtype=jnp.int32)
    kpos_s = jnp.arange(past_len + dec_seq, dtype=jnp.int32)

    def gather_past(cache, j):
        rows = cache[j, page_table]
        return rows.reshape(dec_batch, past_len, *rows.shape[3:])

    xp, xs = x_prompt, x_sample
    kd_p, vd_p, ks_p, vs_p = [], [], [], []
    kd_s, vd_s, ks_s, vs_s = [], [], [], []
    for i in range(DEPTH):
        j = i // 2
        hp = _rms_norm(xp, norm_mix[i])
        hs = _rms_norm(xs, norm_mix[i])
        if i % 2 == 0:
            lam_init = 0.8 - 0.6 * math.exp(-0.3 * i)
            lam = _diff_lambda(lambda_q1[j], lambda_k1[j], lambda_q2[j], lambda_k2[j], lam_init)
            qp, kp, vp = _diff_qkv(hp, w_qkv_diff[j], pos_p)
            qs, kn, vn = _diff_qkv(hs, w_qkv_diff[j], pos_s)
            xp = xp + _diff_attend(qp, pos_p, [kp], [vp], pos_p, lam, subln_diff[j], lam_init, w_o_diff[j])
            xs = xs + _diff_attend(qs, pos_s, [gather_past(cache_k_diff, j), kn], [gather_past(cache_v_diff, j), vn],
                                   kpos_s, lam, subln_diff[j], lam_init, w_o_diff[j])
            kd_p.append(kp); vd_p.append(vp); kd_s.append(kn); vd_s.append(vn)
        else:
            qp, kp, vp = _stick_qkv(hp, w_qkv_stick[j])
            qs, kn, vn = _stick_qkv(hs, w_qkv_stick[j])
            xp = xp + _stick_attend(qp, pos_p, [kp], [vp], pos_p, w_o_stick[j])
            xs = xs + _stick_attend(qs, pos_s, [gather_past(cache_k_stick, j), kn], [gather_past(cache_v_stick, j), vn],
                                    kpos_s, w_o_stick[j])
            ks_p.append(kp); vs_p.append(vp); ks_s.append(kn); vs_s.append(vn)
        xp = xp + _peer(_rms_norm(xp, norm_ffn[i]), peer_w_q[i], peer_sub_keys[i], peer_u[i], peer_v[i])
        xs = xs + _peer(_rms_norm(xs, norm_ffn[i]), peer_w_q[i], peer_sub_keys[i], peer_u[i], peer_v[i])
        xp = _ple(xp, p_prompt[i], norm_ple[i], ple_w_gate[i], ple_w_proj[i])
        xs = _ple(xs, p_sample[i], norm_ple[i], ple_w_gate[i], ple_w_proj[i])
    y_prompt = _rms_norm(xp, norm_final)
    y_sample = _rms_norm(xs, norm_final)
    return (y_prompt, y_sample,
            jnp.stack(kd_p), jnp.stack(vd_p), jnp.stack(ks_p), jnp.stack(vs_p),
            jnp.stack(kd_s), jnp.stack(vd_s), jnp.stack(ks_s), jnp.stack(vs_s))
```

```python
import functools
import math

import jax
import jax.numpy as jnp
from jax import lax
from jax.experimental import pallas as pl
from jax.experimental.pallas import tpu as pltpu

F32 = jnp.float32
BF16 = jnp.bfloat16

D_MODEL = 1024
HEAD_DIM = 64
N_HEADS_A = 8
N_HEADS_B = 16
DV_A = 2 * HEAD_DIM
ROPE_THETA = 10000.0
NORM_EPS = 1e-6
SUBLN_EPS = 1e-5
PAGE_SIZE = 128
PEER_HEADS = 8
PEER_N_KEYS = 128
PEER_TOPK = 16
PEER_HALF = 128
PEER_HK = PEER_HEADS * PEER_TOPK
SCALE = HEAD_DIM ** -0.5

LANES = 128
VMEM_LIMIT = 56 * 1024 * 1024
NEG_INF = float("-inf")


def _cparams(*sem):
    return pltpu.CompilerParams(dimension_semantics=sem, vmem_limit_bytes=VMEM_LIMIT)


def _rms(x, g, eps):
    return x * lax.rsqrt(jnp.mean(x * x, axis=-1, keepdims=True) + eps) * g


def _dot_nt(a, b):
    return lax.dot_general(a, b, (((1,), (1,)), ((), ())), preferred_element_type=F32)


def _row_tile(rows, cap):
    return rows if rows <= cap else cap


def _qkv_body(x_ref, g_ref, w_ref, cos_ref, sin_ref, q_ref, k_ref, v_ref, *, rope):
    h = _rms(x_ref[...], g_ref[...], NORM_EPS).astype(BF16)
    y = jnp.dot(h, w_ref[...], preferred_element_type=F32)
    q, k, v = y[:, :D_MODEL], y[:, D_MODEL:2 * D_MODEL], y[:, 2 * D_MODEL:]
    if rope:
        cos = jnp.tile(cos_ref[...], (1, D_MODEL // LANES))
        sin = jnp.tile(sin_ref[...], (1, D_MODEL // LANES))
        lane = lax.broadcasted_iota(jnp.int32, q.shape, 1)
        first_half = (lane % HEAD_DIM) < (HEAD_DIM // 2)

        def rot(t):
            partner = jnp.where(first_half,
                                pltpu.roll(t, D_MODEL - HEAD_DIM // 2, 1),
                                pltpu.roll(t, HEAD_DIM // 2, 1))
            return t * cos + partner * sin

        q, k = rot(q), rot(k)
    q_ref[...] = q
    k_ref[...] = k
    v_ref[...] = v


def _qkv(x, g, w_bf16, cos, sin, rope):
    rows = x.shape[0]
    tm = _row_tile(rows, 256)
    n_tab = cos.shape[0] // tm
    row_spec = pl.BlockSpec((tm, D_MODEL), lambda i: (i, 0))
    tab_spec = pl.BlockSpec((tm, LANES), lambda i: (i % n_tab, 0))
    out = jax.ShapeDtypeStruct((rows, D_MODEL), F32)
    return pl.pallas_call(
        functools.partial(_qkv_body, rope=rope),
        grid=(rows // tm,),
        in_specs=[row_spec,
                  pl.BlockSpec((1, D_MODEL), lambda i: (0, 0)),
                  pl.BlockSpec((D_MODEL, 3 * D_MODEL), lambda i: (0, 0)),
                  tab_spec, tab_spec],
        out_specs=[row_spec, row_spec, row_spec],
        out_shape=[out, out, out],
        compiler_params=_cparams("parallel"),
        name="qkv_proj",
    )(x, g, w_bf16, cos, sin)


def _rope_tables(pos):
    half = HEAD_DIM // 2
    inv = 1.0 / (ROPE_THETA ** (jnp.arange(half, dtype=F32) * (2.0 / HEAD_DIM)))
    ang = pos.astype(F32)[:, None] * inv[None, :]
    cos, sin = jnp.cos(ang), jnp.sin(ang)
    cos_t = jnp.concatenate([cos, cos, cos, cos], axis=1)
    sin_t = jnp.concatenate([-sin, sin, -sin, sin], axis=1)
    return cos_t, sin_t


def _proj_body(a_ref, w_ref, r_ref, o_ref):
    o_ref[...] = r_ref[...] + jnp.dot(a_ref[...].astype(BF16), w_ref[...],
                                      preferred_element_type=F32)


def _proj_residual(a, w_bf16, res):
    rows = a.shape[0]
    tm = _row_tile(rows, 512)
    row_spec = pl.BlockSpec((tm, D_MODEL), lambda i: (i, 0))
    return pl.pallas_call(
        _proj_body,
        grid=(rows // tm,),
        in_specs=[row_spec, pl.BlockSpec((D_MODEL, D_MODEL), lambda i: (0, 0)), row_spec],
        out_specs=row_spec,
        out_shape=jax.ShapeDtypeStruct((rows, D_MODEL), F32),
        compiler_params=_cparams("parallel"),
        name="out_proj",
    )(a, w_bf16, res)


def _ple_body(x_ref, p_ref, g_ref, wg_ref, wp_ref, gf_ref, o_ref, *, final_norm):
    x = x_ref[...]
    h = _rms(x, g_ref[...], NORM_EPS).astype(BF16)
    gate = jax.nn.sigmoid(jnp.dot(h, wg_ref[...], preferred_element_type=F32))
    proj = jnp.dot(p_ref[...].astype(BF16), wp_ref[...], preferred_element_type=F32)
    y = x + gate * proj
    if final_norm:
        y = _rms(y, gf_ref[...], NORM_EPS)
    o_ref[...] = y


def _ple(x, p, g, wg_bf16, wp_bf16, g_final, final_norm):
    rows = x.shape[0]
    ple_dim = p.shape[1]
    tm = _row_tile(rows, 512)
    row_spec = pl.BlockSpec((tm, D_MODEL), lambda i: (i, 0))
    vec_spec = pl.BlockSpec((1, D_MODEL), lambda i: (0, 0))
    return pl.pallas_call(
        functools.partial(_ple_body, final_norm=final_norm),
        grid=(rows // tm,),
        in_specs=[row_spec, pl.BlockSpec((tm, ple_dim), lambda i: (i, 0)), vec_spec,
                  pl.BlockSpec((D_MODEL, D_MODEL), lambda i: (0, 0)),
                  pl.BlockSpec((ple_dim, D_MODEL), lambda i: (0, 0)), vec_spec],
        out_specs=row_spec,
        out_shape=jax.ShapeDtypeStruct((rows, D_MODEL), F32),
        compiler_params=_cparams("parallel"),
        name="ple",
    )(x, p, g, wg_bf16, wp_bf16, g_final)


def _stack_heads(q):
    lane = lax.broadcasted_iota(jnp.int32, q.shape, 1)
    q0 = jnp.where(lane < HEAD_DIM, q, 0.0)
    q1 = jnp.where(lane >= HEAD_DIM, q, 0.0)
    return jnp.concatenate([q0, q1], axis=0)


def _positions(i, j, tq, tk):
    row = lax.broadcasted_iota(jnp.int32, (2 * tq, tk), 0)
    col = lax.broadcasted_iota(jnp.int32, (2 * tq, tk), 1)
    return i * tq + row % tq, j * tk + col


def _diff_attn_body(q_ref, k_ref, v_ref, sub_ref, lq1_ref, lk1_ref, lq2_ref, lk2_ref, o_ref,
                    qs_sc, m_sc, l_sc, acc_sc, *, tq, lam_init):
    i, j = pl.program_id(2), pl.program_id(3)

    @pl.when(j == 0)
    def _():
        qs_sc[...] = _stack_heads(q_ref[...] * SCALE).astype(BF16)
        m_sc[...] = jnp.full(m_sc.shape, NEG_INF, F32)
        l_sc[...] = jnp.zeros(l_sc.shape, F32)
        acc_sc[...] = jnp.zeros(acc_sc.shape, F32)

    @pl.when(j <= i)
    def _():
        s = _dot_nt(qs_sc[...], k_ref[...].astype(BF16))
        qpos, kpos = _positions(i, j, tq, tq)
        s = jnp.where(kpos <= qpos, s, NEG_INF)
        m_prev = m_sc[...]
        m_new = jnp.maximum(m_prev, jnp.max(s, axis=1, keepdims=True))
        alpha = jnp.exp(m_prev - m_new)
        p = jnp.exp(s - m_new)
        l_sc[...] = alpha * l_sc[...] + jnp.sum(p, axis=1, keepdims=True)
        acc_sc[...] = alpha * acc_sc[...] + jnp.dot(
            p.astype(BF16), v_ref[...].astype(BF16), preferred_element_type=F32)
        m_sc[...] = m_new

    @pl.when(j == i)
    def _():
        lam = (jnp.exp(jnp.sum(lq1_ref[...] * lk1_ref[...], keepdims=True))
               - jnp.exp(jnp.sum(lq2_ref[...] * lk2_ref[...], keepdims=True)) + lam_init)
        a = acc_sc[...] / l_sc[...]
        o = a[:tq] - lam * a[tq:]
        o_ref[...] = _rms(o, sub_ref[...], SUBLN_EPS) * (1.0 - lam_init)


def _diff_attn_prompt(q, k, v, subln, lams, batch, seq, lam_init):
    tq = min(seq, 512)
    nq = seq // tq
    vec64 = pl.BlockSpec((1, HEAD_DIM), lambda b, h, i, j: (0, 0))
    q_spec = pl.BlockSpec((tq, LANES), lambda b, h, i, j: (b * nq + i, h))
    kv_spec = pl.BlockSpec((tq, LANES), lambda b, h, i, j: (b * nq + jnp.minimum(i, j), h))
    return pl.pallas_call(
        functools.partial(_diff_attn_body, tq=tq, lam_init=lam_init),
        grid=(batch, N_HEADS_A, nq, nq),
        in_specs=[q_spec, kv_spec, kv_spec,
                  pl.BlockSpec((1, DV_A), lambda b, h, i, j: (0, 0)),
                  vec64, vec64, vec64, vec64],
        out_specs=q_spec,
        out_shape=jax.ShapeDtypeStruct((batch * seq, D_MODEL), F32),
        scratch_shapes=[pltpu.VMEM((2 * tq, LANES), BF16),
                        pltpu.VMEM((2 * tq, 1), F32),
                        pltpu.VMEM((2 * tq, 1), F32),
                        pltpu.VMEM((2 * tq, LANES), F32)],
        compiler_params=_cparams("parallel", "parallel", "parallel", "arbitrary"),
        name="diff_attn_prompt",
    )(q, k, v, subln, *lams)


def _softplus(z):
    return jnp.maximum(z, 0.0) + jnp.log1p(jnp.exp(-jnp.abs(z)))


def _split_bf16(x):
    hi = x.astype(BF16)
    lo = (x - hi.astype(F32)).astype(BF16)
    return hi, lo


def _stick_attn_body(q_ref, k_ref, v_ref, o_ref, qs_sc, c_sc, acc_sc, tri_sc, *, tq):
    i, jj = pl.program_id(2), pl.program_id(3)

    @pl.when(jj == 0)
    def _():
        qs_sc[...] = _stack_heads(q_ref[...] * SCALE).astype(BF16)
        c_sc[...] = jnp.zeros(c_sc.shape, F32)
        acc_sc[...] = jnp.zeros(acc_sc.shape, F32)
        r = lax.broadcasted_iota(jnp.int32, tri_sc.shape, 0)
        c = lax.broadcasted_iota(jnp.int32, tri_sc.shape, 1)
        tri_sc[...] = (r > c).astype(BF16)

    @pl.when(jj <= i)
    def _():
        z = _dot_nt(qs_sc[...], k_ref[...].astype(BF16))
        qpos, kpos = _positions(i, i - jj, tq, tq)
        mask = kpos < qpos
        sp = _softplus(z)
        log_keep = jnp.where(mask, -sp, 0.0)
        hi, lo = _split_bf16(log_keep)
        tri = tri_sc[...]
        tail = (jnp.dot(hi, tri, preferred_element_type=F32)
                + jnp.dot(lo, tri, preferred_element_type=F32) + c_sc[...])
        w = jnp.where(mask, jnp.exp(z - sp + tail), 0.0)
        c_sc[...] = c_sc[...] + jnp.sum(log_keep, axis=1, keepdims=True)
        acc_sc[...] = acc_sc[...] + jnp.dot(
            w.astype(BF16), v_ref[...].astype(BF16), preferred_element_type=F32)

    @pl.when(jj == i)
    def _():
        a = acc_sc[...]
        lane = lax.broadcasted_iota(jnp.int32, (tq, LANES), 1)
        o_ref[...] = jnp.where(lane < HEAD_DIM, a[:tq], a[tq:])


def _stick_attn_prompt(q, k, v, batch, seq):
    tq = min(seq, 256)
    nq = seq // tq
    q_spec = pl.BlockSpec((tq, LANES), lambda b, h, i, jj: (b * nq + i, h))
    kv_spec = pl.BlockSpec((tq, LANES), lambda b, h, i, jj: (b * nq + jnp.maximum(i - jj, 0), h))
    return pl.pallas_call(
        functools.partial(_stick_attn_body, tq=tq),
        grid=(batch, N_HEADS_B // 2, nq, nq),
        in_specs=[q_spec, kv_spec, kv_spec],
        out_specs=q_spec,
        out_shape=jax.ShapeDtypeStruct((batch * seq, D_MODEL), F32),
        scratch_shapes=[pltpu.VMEM((2 * tq, LANES), BF16),
                        pltpu.VMEM((2 * tq, 1), F32),
                        pltpu.VMEM((2 * tq, LANES), F32),
                        pltpu.VMEM((tq, tq), BF16)],
        compiler_params=_cparams("parallel", "parallel", "parallel", "arbitrary"),
        name="stick_attn_prompt",
    )(q, k, v)


def _topk_axis0(s, payloads, k):
    n = s.shape[0]
    iota = lax.broadcasted_iota(jnp.int32, s.shape, 0)
    vals, picked = [], [[] for _ in payloads]
    for _ in range(k):
        m = jnp.max(s, axis=0, keepdims=True)
        pos = jnp.min(jnp.where(s == m, iota, n), axis=0, keepdims=True)
        hit = iota == pos
        vals.append(m)
        for out, pay in zip(picked, payloads):
            out.append(jnp.sum(jnp.where(hit, pay, 0.0), axis=0, keepdims=True))
        s = jnp.where(hit, NEG_INF, s)
    return jnp.concatenate(vals, axis=0), [jnp.concatenate(p, axis=0) for p in picked]


def _peer_route_body(x_ref, g_ref, wq_ref, sk_ref, a_ref, b_ref, gate_ref,
                     q_sc, a_sc, b_sc, gate_sc):
    tm = x_ref.shape[0]
    h = _rms(x_ref[...], g_ref[...], NORM_EPS).astype(BF16)
    q_sc[...] = jnp.dot(h, wq_ref[...], preferred_element_type=F32).astype(BF16)
    key_id = lax.broadcasted_iota(jnp.int32, (PEER_N_KEYS, tm), 0).astype(F32)

    def head(hd, carry):
        tops = []
        for c in range(2):
            col = pl.multiple_of((hd * 2 + c) * PEER_HALF, PEER_HALF)
            s_t = _dot_nt(sk_ref[hd * 2 + c], q_sc[:, pl.ds(col, PEER_HALF)])
            tops.append(_topk_axis0(s_t, [key_id], PEER_TOPK))
        (s1, (i1,)), (s2, (i2,)) = tops
        shape3 = (PEER_TOPK, PEER_TOPK, tm)
        flat = (PEER_TOPK * PEER_TOPK, tm)
        cand = (s1[:, None, :] + s2[None, :, :]).reshape(flat)
        cand_a = jnp.broadcast_to(i1[:, None, :], shape3).reshape(flat)
        cand_b = jnp.broadcast_to(i2[None, :, :], shape3).reshape(flat)
        top_s, (ea, eb) = _topk_axis0(cand, [cand_a, cand_b], PEER_TOPK)
        e = jnp.exp(top_s - jnp.max(top_s, axis=0, keepdims=True))
        gate = e / jnp.sum(e, axis=0, keepdims=True)
        row = pl.multiple_of(hd * PEER_TOPK, PEER_TOPK)
        a_sc[pl.ds(row, PEER_TOPK), :] = ea
        b_sc[pl.ds(row, PEER_TOPK), :] = eb
        gate_sc[pl.ds(row, PEER_TOPK), :] = gate
        return carry

    lax.fori_loop(0, PEER_HEADS, head, 0)
    a_ref[...] = a_sc[...].T
    b_ref[...] = b_sc[...].T
    gate_ref[...] = gate_sc[...].T


def _peer_route(x, g, wq_bf16, sk_bf16):
    rows = x.shape[0]
    tm = _row_tile(rows, 128)
    qdim = wq_bf16.shape[1]
    row_spec = pl.BlockSpec((tm, PEER_HK), lambda i: (i, 0))
    out = jax.ShapeDtypeStruct((rows, PEER_HK), F32)
    return pl.pallas_call(
        _peer_route_body,
        grid=(rows // tm,),
        in_specs=[pl.BlockSpec((tm, D_MODEL), lambda i: (i, 0)),
                  pl.BlockSpec((1, D_MODEL), lambda i: (0, 0)),
                  pl.BlockSpec((D_MODEL, qdim), lambda i: (0, 0)),
                  pl.BlockSpec(sk_bf16.shape, lambda i: (0, 0, 0))],
        out_specs=[row_spec, row_spec, row_spec],
        out_shape=[out, out, out],
        scratch_shapes=[pltpu.VMEM((tm, qdim), BF16),
                        pltpu.VMEM((PEER_HK, tm), F32),
                        pltpu.VMEM((PEER_HK, tm), F32),
                        pltpu.VMEM((PEER_HK, tm), F32)],
        compiler_params=_cparams("parallel"),
        name="peer_route",
    )(x, g, wq_bf16, sk_bf16)


def _peer_wsel_body(a_ref, b_ref, gate_ref, w_ref):
    tb = a_ref.shape[0]
    sub = lax.broadcasted_iota(jnp.int32, (PEER_N_KEYS, PEER_HK), 0).astype(F32)

    def token(t, carry):
        a = a_ref[pl.ds(t, 1), :]
        b = b_ref[pl.ds(t, 1), :]
        gt = gate_ref[pl.ds(t, 1), :]
        p_t = jnp.where(sub == a, gt, 0.0).astype(BF16)
        q_t = jnp.where(sub == b, 1.0, 0.0).astype(BF16)
        row = pl.multiple_of(t * PEER_N_KEYS, PEER_N_KEYS)
        w_ref[pl.ds(row, PEER_N_KEYS), :] = _dot_nt(p_t, q_t)
        return carry

    lax.fori_loop(0, tb, token, 0)


def _peer_wsel(a, b, gate):
    rows = a.shape[0]
    tb = _row_tile(rows, 64)
    row_spec = pl.BlockSpec((tb, PEER_HK), lambda i: (i, 0))
    w = pl.pallas_call(
        _peer_wsel_body,
        grid=(rows // tb,),
        in_specs=[row_spec, row_spec, row_spec],
        out_specs=pl.BlockSpec((tb * PEER_N_KEYS, PEER_N_KEYS), lambda i: (i, 0)),
        out_shape=jax.ShapeDtypeStruct((rows * PEER_N_KEYS, PEER_N_KEYS), F32),
        compiler_params=_cparams("parallel"),
        name="peer_wsel",
    )(a, b, gate)
    return w.reshape(rows, PEER_N_KEYS, PEER_N_KEYS)


MIX_FIRST_KEYS = 8


def _peer_mix_body(x_ref, g_ref, u_ref, v_ref, w_ref, o_ref, h_sc, acc_sc):
    j = pl.program_id(1)

    @pl.when(j == 0)
    def _():
        h_sc[...] = _rms(x_ref[...], g_ref[...], NORM_EPS).astype(BF16)
        acc_sc[...] = jnp.zeros(acc_sc.shape, F32)

    s = _dot_nt(h_sc[...], u_ref[...])
    act = 0.5 * s * (1.0 + lax.erf(s * math.sqrt(0.5)))
    z = jnp.concatenate(
        [act[:, r * PEER_N_KEYS:(r + 1) * PEER_N_KEYS] * w_ref[:, r, :]
         for r in range(MIX_FIRST_KEYS)], axis=1).astype(BF16)
    acc_sc[...] = acc_sc[...] + jnp.dot(z, v_ref[...], preferred_element_type=F32)

    @pl.when(j == pl.num_programs(1) - 1)
    def _():
        o_ref[...] = x_ref[...] + acc_sc[...]


def _peer_mix(x, g, u_bf16, v_bf16, wsel):
    rows = x.shape[0]
    n_exp = u_bf16.shape[0]
    tm = _row_tile(rows, 512)
    te = MIX_FIRST_KEYS * PEER_N_KEYS
    row_spec = pl.BlockSpec((tm, D_MODEL), lambda i, j: (i, 0))
    exp_spec = pl.BlockSpec((te, D_MODEL), lambda i, j: (j, 0))
    return pl.pallas_call(
        _peer_mix_body,
        grid=(rows // tm, n_exp // te),
        in_specs=[row_spec, pl.BlockSpec((1, D_MODEL), lambda i, j: (0, 0)),
                  exp_spec, exp_spec,
                  pl.BlockSpec((tm, MIX_FIRST_KEYS, PEER_N_KEYS), lambda i, j: (i, j, 0))],
        out_specs=row_spec,
        out_shape=jax.ShapeDtypeStruct((rows, D_MODEL), F32),
        scratch_shapes=[pltpu.VMEM((tm, D_MODEL), BF16), pltpu.VMEM((tm, D_MODEL), F32)],
        compiler_params=_cparams("parallel", "arbitrary"),
        name="peer_mix",
    )(x, g, u_bf16, v_bf16, wsel)


DEC_REP = 8
DEC_ROWS = N_HEADS_B * DEC_REP
PAGES_PER_STEP = 4


def _head_scores(q_ref, kt_ref):
    return jnp.concatenate(
        [jnp.dot(q_ref[h].astype(BF16), kt_ref[h].astype(BF16), preferred_element_type=F32)
         for h in range(N_HEADS_B)], axis=0)


def _diff_decode_body(pt_ref, q_ref, knew_ref, vnew_ref, sub_ref, lq1_ref, lk1_ref, lq2_ref, lk2_ref,
                      *rest, lam_init):
    kt_refs = rest[:PAGES_PER_STEP]
    v_refs = rest[PAGES_PER_STEP:2 * PAGES_PER_STEP]
    o_ref, m_sc, l_sc, acc_sc = rest[2 * PAGES_PER_STEP:]
    step = pl.program_id(1)

    @pl.when(step == 0)
    def _():
        m_sc[...] = jnp.full(m_sc.shape, NEG_INF, F32)
        l_sc[...] = jnp.zeros(l_sc.shape, F32)
        acc_sc[...] = jnp.zeros(acc_sc.shape, F32)

    s = jnp.concatenate([_head_scores(q_ref, kt) for kt in kt_refs], axis=1) * SCALE
    m_prev = m_sc[...]
    m_new = jnp.maximum(m_prev, jnp.max(s, axis=1, keepdims=True))
    alpha = jnp.exp(m_prev - m_new)
    p = jnp.exp(s - m_new)
    l_sc[...] = alpha * l_sc[...] + jnp.sum(p, axis=1, keepdims=True)
    m_sc[...] = m_new
    p = p.astype(BF16)
    pv = []
    for h in range(N_HEADS_A):
        rows = slice(2 * DEC_REP * h, 2 * DEC_REP * (h + 1))
        acc_h = jnp.zeros((2 * DEC_REP, DV_A), F32)
        for pg, v_ref in enumerate(v_refs):
            v_h = v_ref[pl.ds(h, PAGE_SIZE, stride=N_HEADS_A), :].astype(BF16)
            acc_h = acc_h + jnp.dot(p[rows, pg * PAGE_SIZE:(pg + 1) * PAGE_SIZE], v_h,
                                    preferred_element_type=F32)
        pv.append(acc_h)
    acc_sc[...] = alpha * acc_sc[...] + jnp.concatenate(pv, axis=0)

    @pl.when(step == pl.num_programs(1) - 1)
    def _():
        s_new = jnp.sum(q_ref[...] * knew_ref[...], axis=-1,
                        keepdims=True).reshape(DEC_ROWS, 1) * SCALE
        m_f = jnp.maximum(m_sc[...], s_new)
        a = jnp.exp(m_sc[...] - m_f)
        p_new = jnp.exp(s_new - m_f)
        l_f = a * l_sc[...] + p_new
        o = (a * acc_sc[...] + p_new * vnew_ref[...]) / l_f
        lam = (jnp.exp(jnp.sum(lq1_ref[...] * lk1_ref[...], keepdims=True))
               - jnp.exp(jnp.sum(lq2_ref[...] * lk2_ref[...], keepdims=True)) + lam_init)
        for h in range(N_HEADS_A):
            r0 = 2 * DEC_REP * h
            o_h = o[r0:r0 + DEC_REP] - lam * o[r0 + DEC_REP:r0 + 2 * DEC_REP]
            o_ref[h * DEC_REP:(h + 1) * DEC_REP, :] = (
                _rms(o_h, sub_ref[...], SUBLN_EPS) * (1.0 - lam_init))


def _diff_decode(q, k_new, v_new, cache_kt, cache_v, page_table, subln, lams, lam_init):
    batch, n_pages = page_table.shape
    steps = n_pages // PAGES_PER_STEP
    rep = lambda t: jnp.broadcast_to(t.reshape(batch, N_HEADS_B, 1, HEAD_DIM),
                                     (batch, N_HEADS_B, DEC_REP, HEAD_DIM))
    v_rep = jnp.broadcast_to(v_new.reshape(batch, N_HEADS_A, 1, DV_A),
                             (batch, N_HEADS_A, 2 * DEC_REP, DV_A)).reshape(batch, DEC_ROWS, DV_A)
    q_spec = pl.BlockSpec((None, N_HEADS_B, DEC_REP, HEAD_DIM), lambda b, s, pt: (b, 0, 0, 0))
    vec64 = pl.BlockSpec((1, HEAD_DIM), lambda b, s, pt: (0, 0))

    def page(pg):
        return lambda b, s, pt: (pt[b, s * PAGES_PER_STEP + pg], 0, 0, 0)

    def page3(pg):
        return lambda b, s, pt: (pt[b, s * PAGES_PER_STEP + pg], 0, 0)

    kt_specs = [pl.BlockSpec((None, N_HEADS_B, HEAD_DIM, PAGE_SIZE), page(pg))
                for pg in range(PAGES_PER_STEP)]
    v_specs = [pl.BlockSpec((None, PAGE_SIZE * N_HEADS_A, DV_A), page3(pg))
               for pg in range(PAGES_PER_STEP)]
    out = pl.pallas_call(
        functools.partial(_diff_decode_body, lam_init=lam_init),
        grid_spec=pltpu.PrefetchScalarGridSpec(
            num_scalar_prefetch=1,
            grid=(batch, steps),
            in_specs=[q_spec, q_spec,
                      pl.BlockSpec((None, DEC_ROWS, DV_A), lambda b, s, pt: (b, 0, 0)),
                      pl.BlockSpec((1, DV_A), lambda b, s, pt: (0, 0)),
                      vec64, vec64, vec64, vec64] + kt_specs + v_specs,
            out_specs=pl.BlockSpec((None, N_HEADS_A * DEC_REP, DV_A), lambda b, s, pt: (b, 0, 0)),
            scratch_shapes=[pltpu.VMEM((DEC_ROWS, 1), F32),
                            pltpu.VMEM((DEC_ROWS, 1), F32),
                            pltpu.VMEM((DEC_ROWS, DV_A), F32)]),
        out_shape=jax.ShapeDtypeStruct((batch, N_HEADS_A * DEC_REP, DV_A), F32),
        compiler_params=_cparams("parallel", "arbitrary"),
        name="diff_attn_decode",
    )(page_table, rep(q), rep(k_new), v_rep, subln, *lams,
      *([cache_kt] * PAGES_PER_STEP), *([cache_v] * PAGES_PER_STEP))
    return out[:, ::DEC_REP, :].reshape(batch, D_MODEL)


def _stick_decode_body(pt_ref, q_ref, *rest):
    kt_refs = rest[:PAGES_PER_STEP]
    vt_refs = rest[PAGES_PER_STEP:2 * PAGES_PER_STEP]
    o_ref, c_sc, acc_sc = rest[2 * PAGES_PER_STEP:]
    step = pl.program_id(1)

    @pl.when(step == 0)
    def _():
        c_sc[...] = jnp.zeros(c_sc.shape, F32)
        acc_sc[...] = jnp.zeros(acc_sc.shape, F32)

    r = lax.broadcasted_iota(jnp.int32, (PAGE_SIZE, PAGE_SIZE), 0)
    c = lax.broadcasted_iota(jnp.int32, (PAGE_SIZE, PAGE_SIZE), 1)
    tri = (r > c).astype(BF16)
    carry = c_sc[...]
    acc = acc_sc[...]
    for kt_ref, vt_ref in zip(kt_refs, vt_refs):
        z = _head_scores(q_ref, kt_ref) * SCALE
        sp = _softplus(z)
        hi, lo = _split_bf16(-sp)
        tail = (jnp.dot(hi, tri, preferred_element_type=F32)
                + jnp.dot(lo, tri, preferred_element_type=F32) + carry)
        w = jnp.exp(z - sp + tail).astype(BF16)
        carry = carry - jnp.sum(sp, axis=1, keepdims=True)
        acc = acc + jnp.concatenate(
            [_dot_nt(w[h * DEC_REP:(h + 1) * DEC_REP], vt_ref[h].astype(BF16))
             for h in range(N_HEADS_B)], axis=0)
    c_sc[...] = carry
    acc_sc[...] = acc

    @pl.when(step == pl.num_programs(1) - 1)
    def _():
        o_ref[...] = acc_sc[...]


def _stick_decode(q, cache_kt, cache_vt, page_table):
    batch, n_pages = page_table.shape
    steps = n_pages // PAGES_PER_STEP
    q_rep = jnp.broadcast_to(q.reshape(batch, N_HEADS_B, 1, HEAD_DIM),
                             (batch, N_HEADS_B, DEC_REP, HEAD_DIM))

    def page(pg):
        return lambda b, s, pt: (pt[b, n_pages - 1 - (s * PAGES_PER_STEP + pg)], 0, 0, 0)

    specs = [pl.BlockSpec((None, N_HEADS_B, HEAD_DIM, PAGE_SIZE), page(pg))
             for pg in range(PAGES_PER_STEP)]
    out = pl.pallas_call(
        _stick_decode_body,
        grid_spec=pltpu.PrefetchScalarGridSpec(
            num_scalar_prefetch=1,
            grid=(batch, steps),
            in_specs=[pl.BlockSpec((None, N_HEADS_B, DEC_REP, HEAD_DIM),
                                   lambda b, s, pt: (b, 0, 0, 0))] + specs + specs,
            out_specs=pl.BlockSpec((None, DEC_ROWS, HEAD_DIM), lambda b, s, pt: (b, 0, 0)),
            scratch_shapes=[pltpu.VMEM((DEC_ROWS, 1), F32),
                            pltpu.VMEM((DEC_ROWS, HEAD_DIM), F32)]),
        out_shape=jax.ShapeDtypeStruct((batch, DEC_ROWS, HEAD_DIM), F32),
        compiler_params=_cparams("parallel", "arbitrary"),
        name="stick_attn_decode",
    )(page_table, q_rep, *([cache_kt] * PAGES_PER_STEP), *([cache_vt] * PAGES_PER_STEP))
    return out[:, ::DEC_REP, :].reshape(batch, D_MODEL)


def _keys_last(cache):
    return jnp.transpose(cache, (0, 2, 3, 1))


def kernel(x_prompt, x_sample, p_prompt, p_sample, cache_k_diff, cache_v_diff, cache_k_stick, cache_v_stick, page_table, norm_mix, norm_ffn, norm_ple, norm_final, w_qkv_diff, w_o_diff, lambda_q1, lambda_k1, lambda_q2, lambda_k2, subln_diff, w_qkv_stick, w_o_stick, peer_w_q, peer_sub_keys, peer_u, peer_v, ple_w_gate, ple_w_proj):
    batch, seq, _ = x_prompt.shape
    dec_batch, dec_seq, _ = x_sample.shape
    assert dec_seq == 1
    depth = norm_mix.shape[0]
    n_pages = page_table.shape[1]
    past_len = n_pages * PAGE_SIZE
    n_p, n_s = batch * seq, dec_batch * dec_seq

    xp = x_prompt.reshape(n_p, D_MODEL)
    xs = x_sample.reshape(n_s, D_MODEL)
    cos_p, sin_p = _rope_tables(jnp.arange(seq, dtype=jnp.int32))
    cos_s, sin_s = _rope_tables(jnp.full((n_s,), past_len, jnp.int32))
    vec = lambda t: t.reshape(1, -1)
    g_final = vec(norm_final)

    new_kv = {}
    for i in range(depth):
        j = i // 2
        g_mix, g_ffn, g_ple = vec(norm_mix[i]), vec(norm_ffn[i]), vec(norm_ple[i])
        if i % 2 == 0:
            lam_init = 0.8 - 0.6 * math.exp(-0.3 * i)
            lams = [vec(lambda_q1[j]), vec(lambda_k1[j]), vec(lambda_q2[j]), vec(lambda_k2[j])]
            sub = vec(subln_diff[j])
            w_qkv = w_qkv_diff[j].astype(BF16)
            w_o = w_o_diff[j].astype(BF16)
            qp, kp, vp = _qkv(xp, g_mix, w_qkv, cos_p, sin_p, True)
            qs, kn, vn = _qkv(xs, g_mix, w_qkv, cos_s, sin_s, True)
            op = _diff_attn_prompt(qp, kp, vp, sub, lams, batch, seq, lam_init)
            os_ = _diff_decode(qs, kn, vn, _keys_last(cache_k_diff[j]),
                               cache_v_diff[j].reshape(-1, PAGE_SIZE * N_HEADS_A, DV_A),
                               page_table, sub, lams, lam_init)
            heads_k, heads_v = (2 * N_HEADS_A, HEAD_DIM), (N_HEADS_A, DV_A)
            tag = "diff"
        else:
            w_qkv = w_qkv_stick[j].astype(BF16)
            w_o = w_o_stick[j].astype(BF16)
            qp, kp, vp = _qkv(xp, g_mix, w_qkv, cos_p, sin_p, False)
            qs, kn, vn = _qkv(xs, g_mix, w_qkv, cos_s, sin_s, False)
            op = _stick_attn_prompt(qp, kp, vp, batch, seq)
            os_ = _stick_decode(qs, _keys_last(cache_k_stick[j]), _keys_last(cache_v_stick[j]),
                                page_table)
            heads_k = heads_v = (N_HEADS_B, HEAD_DIM)
            tag = "stick"
        new_kv.setdefault(tag, []).append((
            kp.reshape(batch, seq, *heads_k), vp.reshape(batch, seq, *heads_v),
            kn.reshape(dec_batch, dec_seq, *heads_k), vn.reshape(dec_batch, dec_seq, *heads_v)))
        xp = _proj_residual(op, w_o, xp)
        xs = _proj_residual(os_, w_o, xs)

        wq = peer_w_q[i].astype(BF16)
        sk = peer_sub_keys[i].reshape(2 * PEER_HEADS, PEER_N_KEYS, PEER_HALF).astype(BF16)
        u, v = peer_u[i].astype(BF16), peer_v[i].astype(BF16)
        xp = _peer_mix(xp, g_ffn, u, v, _peer_wsel(*_peer_route(xp, g_ffn, wq, sk)))
        xs = _peer_mix(xs, g_ffn, u, v, _peer_wsel(*_peer_route(xs, g_ffn, wq, sk)))

        wg, wp = ple_w_gate[i].astype(BF16), ple_w_proj[i].astype(BF16)
        last = i == depth - 1
        xp = _ple(xp, p_prompt[i].reshape(n_p, -1), g_ple, wg, wp, g_final, last)
        xs = _ple(xs, p_sample[i].reshape(n_s, -1), g_ple, wg, wp, g_final, last)

    stack = lambda tag, idx: jnp.stack([t[idx] for t in new_kv[tag]])
    return (xp.reshape(batch, seq, D_MODEL), xs.reshape(dec_batch, dec_seq, D_MODEL),
            stack("diff", 0), stack("diff", 1), stack("stick", 0), stack("stick", 1),
            stack("diff", 2), stack("diff", 3), stack("stick", 2), stack("stick", 3))
```

```python
import functools
import math

import jax
import jax.numpy as jnp
import numpy as np
from jax import lax
from jax.experimental import pallas as pl
from jax.experimental.pallas import tpu as pltpu

F32 = jnp.float32
BF16 = jnp.bfloat16

D_MODEL = 1024
HEAD_DIM = 64
N_HEADS_A = 8
N_HEADS_B = 16
DV_A = 2 * HEAD_DIM
ROPE_THETA = 10000.0
NORM_EPS = 1e-6
SUBLN_EPS = 1e-5
PAGE_SIZE = 128
PEER_HEADS = 8
PEER_N_KEYS = 128
PEER_TOPK = 16
PEER_HALF = 128
PEER_HK = PEER_HEADS * PEER_TOPK
SCALE = HEAD_DIM ** -0.5

LANES = 128
VMEM_LIMIT = 56 * 1024 * 1024
NEG_INF = float("-inf")


def _cparams(*sem):
    return pltpu.CompilerParams(dimension_semantics=sem, vmem_limit_bytes=VMEM_LIMIT)


def _rms(x, g, eps):
    return x * lax.rsqrt(jnp.mean(x * x, axis=-1, keepdims=True) + eps) * g


def _dot_nt(a, b):
    return lax.dot_general(a, b, (((1,), (1,)), ((), ())), preferred_element_type=F32)


def _row_tile(rows, cap):
    return rows if rows <= cap else cap


def _qkv_body(x_ref, g_ref, w_ref, cos_ref, sin_ref, q_ref, k_ref, v_ref, *, rope):
    h = _rms(x_ref[...], g_ref[...], NORM_EPS).astype(BF16)
    y = jnp.dot(h, w_ref[...], preferred_element_type=F32)
    q, k, v = y[:, :D_MODEL], y[:, D_MODEL:2 * D_MODEL], y[:, 2 * D_MODEL:]
    if rope:
        cos = jnp.tile(cos_ref[...], (1, D_MODEL // LANES))
        sin = jnp.tile(sin_ref[...], (1, D_MODEL // LANES))
        lane = lax.broadcasted_iota(jnp.int32, q.shape, 1)
        first_half = (lane % HEAD_DIM) < (HEAD_DIM // 2)

        def rot(t):
            partner = jnp.where(first_half,
                                pltpu.roll(t, D_MODEL - HEAD_DIM // 2, 1),
                                pltpu.roll(t, HEAD_DIM // 2, 1))
            return t * cos + partner * sin

        q, k = rot(q), rot(k)
    q_ref[...] = q
    k_ref[...] = k
    v_ref[...] = v


def _qkv(x, g, w_bf16, cos, sin, rope):
    rows = x.shape[0]
    tm = _row_tile(rows, 256)
    n_tab = cos.shape[0] // tm
    row_spec = pl.BlockSpec((tm, D_MODEL), lambda i: (i, 0))
    tab_spec = pl.BlockSpec((tm, LANES), lambda i: (i % n_tab, 0))
    out = jax.ShapeDtypeStruct((rows, D_MODEL), F32)
    return pl.pallas_call(
        functools.partial(_qkv_body, rope=rope),
        grid=(rows // tm,),
        in_specs=[row_spec,
                  pl.BlockSpec((1, D_MODEL), lambda i: (0, 0)),
                  pl.BlockSpec((D_MODEL, 3 * D_MODEL), lambda i: (0, 0)),
                  tab_spec, tab_spec],
        out_specs=[row_spec, row_spec, row_spec],
        out_shape=[out, out, out],
        compiler_params=_cparams("parallel"),
        name="qkv_proj",
    )(x, g, w_bf16, cos, sin)


def _rope_tables(pos):
    half = HEAD_DIM // 2
    inv = 1.0 / (ROPE_THETA ** (jnp.arange(half, dtype=F32) * (2.0 / HEAD_DIM)))
    ang = pos.astype(F32)[:, None] * inv[None, :]
    cos, sin = jnp.cos(ang), jnp.sin(ang)
    cos_t = jnp.concatenate([cos, cos, cos, cos], axis=1)
    sin_t = jnp.concatenate([-sin, sin, -sin, sin], axis=1)
    return cos_t, sin_t


def _proj_body(a_ref, w_ref, r_ref, o_ref):
    o_ref[...] = r_ref[...] + jnp.dot(a_ref[...].astype(BF16), w_ref[...],
                                      preferred_element_type=F32)


def _proj_residual(a, w_bf16, res):
    rows = a.shape[0]
    tm = _row_tile(rows, 512)
    row_spec = pl.BlockSpec((tm, D_MODEL), lambda i: (i, 0))
    return pl.pallas_call(
        _proj_body,
        grid=(rows // tm,),
        in_specs=[row_spec, pl.BlockSpec((D_MODEL, D_MODEL), lambda i: (0, 0)), row_spec],
        out_specs=row_spec,
        out_shape=jax.ShapeDtypeStruct((rows, D_MODEL), F32),
        compiler_params=_cparams("parallel"),
        name="out_proj",
    )(a, w_bf16, res)


def _ple_body(x_ref, p_ref, g_ref, wg_ref, wp_ref, gf_ref, o_ref, *, final_norm):
    x = x_ref[...]
    h = _rms(x, g_ref[...], NORM_EPS).astype(BF16)
    gate = jax.nn.sigmoid(jnp.dot(h, wg_ref[...], preferred_element_type=F32))
    proj = jnp.dot(p_ref[...].astype(BF16), wp_ref[...], preferred_element_type=F32)
    y = x + gate * proj
    if final_norm:
        y = _rms(y, gf_ref[...], NORM_EPS)
    o_ref[...] = y


def _ple(x, p, g, wg_bf16, wp_bf16, g_final, final_norm):
    rows = x.shape[0]
    ple_dim = p.shape[1]
    tm = _row_tile(rows, 512)
    row_spec = pl.BlockSpec((tm, D_MODEL), lambda i: (i, 0))
    vec_spec = pl.BlockSpec((1, D_MODEL), lambda i: (0, 0))
    return pl.pallas_call(
        functools.partial(_ple_body, final_norm=final_norm),
        grid=(rows // tm,),
        in_specs=[row_spec, pl.BlockSpec((tm, ple_dim), lambda i: (i, 0)), vec_spec,
                  pl.BlockSpec((D_MODEL, D_MODEL), lambda i: (0, 0)),
                  pl.BlockSpec((ple_dim, D_MODEL), lambda i: (0, 0)), vec_spec],
        out_specs=row_spec,
        out_shape=jax.ShapeDtypeStruct((rows, D_MODEL), F32),
        compiler_params=_cparams("parallel"),
        name="ple",
    )(x, p, g, wg_bf16, wp_bf16, g_final)


def _stack_heads(q):
    lane = lax.broadcasted_iota(jnp.int32, q.shape, 1)
    q0 = jnp.where(lane < HEAD_DIM, q, 0.0)
    q1 = jnp.where(lane >= HEAD_DIM, q, 0.0)
    return jnp.concatenate([q0, q1], axis=0)


def _causal_steps(nq, descending):
    qi, kj = [], []
    for i in range(nq):
        for j in (range(i, -1, -1) if descending else range(i + 1)):
            qi.append(i)
            kj.append(j)
    return np.asarray(qi, np.int32), np.asarray(kj, np.int32)


def _diag_mask(tq, strict):
    row = lax.broadcasted_iota(jnp.int32, (2 * tq, tq), 0) & (tq - 1)
    col = lax.broadcasted_iota(jnp.int32, (2 * tq, tq), 1)
    return col < row if strict else col <= row


def _across(stat, width):
    return jnp.tile(stat, (1, width // LANES))


def _attn_tile(seq, cap):
    tq = min(seq, cap)
    assert seq % tq == 0 and tq & (tq - 1) == 0 and tq % LANES == 0
    return tq


def _diff_attn_body(qi_ref, kj_ref, q_ref, k_ref, v_ref, sub_ref, lq1_ref, lk1_ref, lq2_ref,
                    lk2_ref, o_ref, qs_sc, m_sc, l_sc, acc_sc, *, tq, lam_init):
    t = pl.program_id(2)
    i, j = qi_ref[t], kj_ref[t]

    @pl.when(j == 0)
    def _():
        qs_sc[...] = _stack_heads(q_ref[...] * SCALE).astype(BF16)
        m_sc[...] = jnp.full(m_sc.shape, NEG_INF, F32)
        l_sc[...] = jnp.zeros(l_sc.shape, F32)
        acc_sc[...] = jnp.zeros(acc_sc.shape, F32)

    def update(diagonal):
        s = _dot_nt(qs_sc[...], k_ref[...].astype(BF16))
        if diagonal:
            s = jnp.where(_diag_mask(tq, strict=False), s, NEG_INF)
        m_prev = m_sc[...]
        m_new = jnp.maximum(m_prev, jnp.max(s, axis=1, keepdims=True))
        alpha = jnp.exp(m_prev - m_new)
        p = jnp.exp(s - _across(m_new, tq))
        l_sc[...] = alpha * l_sc[...] + jnp.sum(p, axis=1, keepdims=True)
        acc_sc[...] = alpha * acc_sc[...] + jnp.dot(
            p.astype(BF16), v_ref[...].astype(BF16), preferred_element_type=F32)
        m_sc[...] = m_new

    @pl.when(j < i)
    def _():
        update(False)

    @pl.when(j == i)
    def _():
        update(True)
        lam = (jnp.exp(jnp.sum(lq1_ref[...] * lk1_ref[...], keepdims=True))
               - jnp.exp(jnp.sum(lq2_ref[...] * lk2_ref[...], keepdims=True)) + lam_init)
        a = acc_sc[...] / l_sc[...]
        o = a[:tq] - lam * a[tq:]
        o_ref[...] = _rms(o, sub_ref[...], SUBLN_EPS) * (1.0 - lam_init)


def _diff_attn_prompt(q, k, v, subln, lams, batch, seq, lam_init):
    tq = _attn_tile(seq, 512)
    nq = seq // tq
    qi, kj = _causal_steps(nq, descending=False)
    const = lambda shape: pl.BlockSpec(shape, lambda b, h, t, qi, kj: (0, 0))
    q_spec = pl.BlockSpec((tq, LANES), lambda b, h, t, qi, kj: (b * nq + qi[t], h))
    kv_spec = pl.BlockSpec((tq, LANES), lambda b, h, t, qi, kj: (b * nq + kj[t], h))
    vec64 = const((1, HEAD_DIM))
    return pl.pallas_call(
        functools.partial(_diff_attn_body, tq=tq, lam_init=lam_init),
        grid_spec=pltpu.PrefetchScalarGridSpec(
            num_scalar_prefetch=2,
            grid=(batch, N_HEADS_A, len(qi)),
            in_specs=[q_spec, kv_spec, kv_spec, const((1, DV_A)), vec64, vec64, vec64, vec64],
            out_specs=q_spec,
            scratch_shapes=[pltpu.VMEM((2 * tq, LANES), BF16),
                            pltpu.VMEM((2 * tq, LANES), F32),
                            pltpu.VMEM((2 * tq, LANES), F32),
                            pltpu.VMEM((2 * tq, LANES), F32)]),
        out_shape=jax.ShapeDtypeStruct((batch * seq, D_MODEL), F32),
        compiler_params=_cparams("parallel", "parallel", "arbitrary"),
        name="diff_attn_prompt",
    )(qi, kj, q, k, v, subln, *lams)


def _softplus(z):
    return jnp.maximum(z, 0.0) + jnp.log(1.0 + jnp.exp(jnp.minimum(z, -z)))


def _split_bf16(x):
    hi = x.astype(BF16)
    lo = (x - hi.astype(F32)).astype(BF16)
    return hi, lo


def _neg_tri2(n):
    r = lax.broadcasted_iota(jnp.int32, (2 * n, n), 0) & (n - 1)
    c = lax.broadcasted_iota(jnp.int32, (2 * n, n), 1)
    return jnp.where(r > c, -1.0, 0.0).astype(BF16)


def _later_keys_tail(cost, carry, tri2):
    hi, lo = _split_bf16(cost)
    later = jnp.dot(jnp.concatenate([hi, lo], axis=1), tri2, preferred_element_type=F32)
    return later + _across(carry, cost.shape[1])


STICK_SUB = 256


def _stick_attn_body(qi_ref, kj_ref, q_ref, k_ref, v_ref, o_ref, qs_sc, c_sc, acc_sc, tri_sc, *, tq):
    t = pl.program_id(2)
    i, j = qi_ref[t], kj_ref[t]
    sub = tri_sc.shape[1]

    @pl.when(j == i)
    def _():
        qs_sc[...] = _stack_heads(q_ref[...] * SCALE).astype(BF16)
        c_sc[...] = jnp.zeros(c_sc.shape, F32)
        acc_sc[...] = jnp.zeros(acc_sc.shape, F32)
        tri_sc[...] = _neg_tri2(sub)

    def update(diagonal):
        z = _dot_nt(qs_sc[...], k_ref[...].astype(BF16))
        sp = _softplus(z)
        if diagonal:
            mask = _diag_mask(tq, strict=True)
            cost = jnp.where(mask, sp, 0.0)
        else:
            cost = sp
        carry = c_sc[...]
        tails = []
        for blk in reversed(range(tq // sub)):
            cost_b = cost[:, blk * sub:(blk + 1) * sub]
            tails.append(_later_keys_tail(cost_b, carry, tri_sc[...]))
            carry = carry - jnp.sum(cost_b, axis=1, keepdims=True)
        c_sc[...] = carry
        w = jnp.exp(z - sp + jnp.concatenate(tails[::-1], axis=1))
        if diagonal:
            w = jnp.where(mask, w, 0.0)
        acc_sc[...] = acc_sc[...] + jnp.dot(
            w.astype(BF16), v_ref[...].astype(BF16), preferred_element_type=F32)

    @pl.when(j < i)
    def _():
        update(False)

    @pl.when(j == i)
    def _():
        update(True)

    @pl.when(j == 0)
    def _():
        a = acc_sc[...]
        lane = lax.broadcasted_iota(jnp.int32, (tq, LANES), 1)
        o_ref[...] = jnp.where(lane < HEAD_DIM, a[:tq], a[tq:])


def _stick_attn_prompt(q, k, v, batch, seq):
    tq = _attn_tile(seq, 512)
    sub = min(tq, STICK_SUB)
    nq = seq // tq
    qi, kj = _causal_steps(nq, descending=True)
    q_spec = pl.BlockSpec((tq, LANES), lambda b, h, t, qi, kj: (b * nq + qi[t], h))
    kv_spec = pl.BlockSpec((tq, LANES), lambda b, h, t, qi, kj: (b * nq + kj[t], h))
    return pl.pallas_call(
        functools.partial(_stick_attn_body, tq=tq),
        grid_spec=pltpu.PrefetchScalarGridSpec(
            num_scalar_prefetch=2,
            grid=(batch, N_HEADS_B // 2, len(qi)),
            in_specs=[q_spec, kv_spec, kv_spec],
            out_specs=q_spec,
            scratch_shapes=[pltpu.VMEM((2 * tq, LANES), BF16),
                            pltpu.VMEM((2 * tq, LANES), F32),
                            pltpu.VMEM((2 * tq, LANES), F32),
                            pltpu.VMEM((2 * sub, sub), BF16)]),
        out_shape=jax.ShapeDtypeStruct((batch * seq, D_MODEL), F32),
        compiler_params=_cparams("parallel", "parallel", "arbitrary"),
        name="stick_attn_prompt",
    )(qi, kj, q, k, v)


def _topk_axis0(s, order, payloads, k):
    big = jnp.iinfo(jnp.int32).max if order.dtype == jnp.int32 else jnp.inf
    vals, wins, picked = [], [], [[] for _ in payloads]
    for _ in range(k):
        m = jnp.max(s, axis=0, keepdims=True)
        win = jnp.min(jnp.where(s == m, order, big), axis=0, keepdims=True)
        hit = order == win
        vals.append(m)
        wins.append(win)
        for out, pay in zip(picked, payloads):
            out.append(jnp.sum(jnp.where(hit, pay, 0.0), axis=0, keepdims=True))
        s = jnp.where(hit, NEG_INF, s)
    cat = lambda rows: jnp.concatenate(rows, axis=0)
    return cat(vals), cat(wins), [cat(p) for p in picked]


SUB = 8


def _pair_candidates():
    pairs = [(0, j) for j in range(SUB)] + [(0, j) for j in range(SUB, 2 * SUB)]
    for i in range(1, SUB):
        pairs += [(i, j) for j in range(SUB)]
    pairs += [(i, 0) for i in range(SUB, 2 * SUB)]
    flat = np.asarray([i * PEER_TOPK + j for i, j in pairs], np.float32)
    bias = np.asarray([0.0 if (i + 1) * (j + 1) <= PEER_TOPK else NEG_INF for i, j in pairs],
                      np.float32)
    return flat, bias


def _by_first_rank(x):
    t = x.shape[1]
    rep = lambda i: jnp.broadcast_to(x[i:i + 1], (SUB, t))
    return jnp.concatenate([rep(0)] + [rep(i) for i in range(SUB)] + [x[SUB:]], axis=0)


def _by_second_rank(y):
    t = y.shape[1]
    return jnp.concatenate([y[:SUB], y[SUB:]] + [y[:SUB]] * (SUB - 1)
                           + [jnp.broadcast_to(y[0:1], (SUB, t))], axis=0)


def _peer_route_body(x_ref, g_ref, wq_ref, sk_ref, flat_ref, bias_ref, a_ref, b_ref, gate_ref,
                     q_sc, a_sc, b_sc, gate_sc):
    tm = x_ref.shape[0]
    h = _rms(x_ref[...], g_ref[...], NORM_EPS).astype(BF16)
    q_sc[...] = jnp.dot(h, wq_ref[...], preferred_element_type=F32).astype(BF16)
    key_id = lax.broadcasted_iota(jnp.int32, (PEER_N_KEYS, tm), 0)

    def head(hd, carry):
        tops = []
        for c in range(2):
            col = pl.multiple_of((hd * 2 + c) * PEER_HALF, PEER_HALF)
            s_t = _dot_nt(sk_ref[hd * 2 + c], q_sc[:, pl.ds(col, PEER_HALF)])
            vals, ids, _ = _topk_axis0(s_t, key_id, [], PEER_TOPK)
            tops.append((vals, ids.astype(F32)))
        (s1, i1), (s2, i2) = tops
        cand = _by_first_rank(s1) + _by_second_rank(s2) + bias_ref[...]
        top_s, _, (ea, eb) = _topk_axis0(cand, flat_ref[...],
                                         [_by_first_rank(i1), _by_second_rank(i2)], PEER_TOPK)
        e = jnp.exp(top_s - jnp.max(top_s, axis=0, keepdims=True))
        gate = e / jnp.sum(e, axis=0, keepdims=True)
        row = pl.multiple_of(hd * PEER_TOPK, PEER_TOPK)
        a_sc[pl.ds(row, PEER_TOPK), :] = ea
        b_sc[pl.ds(row, PEER_TOPK), :] = eb
        gate_sc[pl.ds(row, PEER_TOPK), :] = gate
        return carry

    lax.fori_loop(0, PEER_HEADS, head, 0, unroll=2)
    a_ref[...] = a_sc[...].T
    b_ref[...] = b_sc[...].T
    gate_ref[...] = gate_sc[...].T


def _peer_route(x, g, wq_bf16, sk_bf16):
    rows = x.shape[0]
    tm = _row_tile(rows, 128)
    qdim = wq_bf16.shape[1]
    row_spec = pl.BlockSpec((tm, PEER_HK), lambda i: (i, 0))
    out = jax.ShapeDtypeStruct((rows, PEER_HK), F32)
    flat, bias = (np.ascontiguousarray(np.broadcast_to(t[:, None], (t.shape[0], tm)))
                  for t in _pair_candidates())
    tab_spec = pl.BlockSpec(flat.shape, lambda i: (0, 0))
    return pl.pallas_call(
        _peer_route_body,
        grid=(rows // tm,),
        in_specs=[pl.BlockSpec((tm, D_MODEL), lambda i: (i, 0)),
                  pl.BlockSpec((1, D_MODEL), lambda i: (0, 0)),
                  pl.BlockSpec((D_MODEL, qdim), lambda i: (0, 0)),
                  pl.BlockSpec(sk_bf16.shape, lambda i: (0, 0, 0)),
                  tab_spec, tab_spec],
        out_specs=[row_spec, row_spec, row_spec],
        out_shape=[out, out, out],
        scratch_shapes=[pltpu.VMEM((tm, qdim), BF16),
                        pltpu.VMEM((PEER_HK, tm), F32),
                        pltpu.VMEM((PEER_HK, tm), F32),
                        pltpu.VMEM((PEER_HK, tm), F32)],
        compiler_params=_cparams("parallel"),
        name="peer_route",
    )(x, g, wq_bf16, sk_bf16, flat, bias)


def _peer_wsel_body(a_ref, b_ref, gate_ref, w_ref):
    tb = a_ref.shape[0]
    sub = lax.broadcasted_iota(jnp.int32, (PEER_N_KEYS, PEER_HK), 0).astype(F32)

    def token(t, carry):
        a = a_ref[pl.ds(t, 1), :]
        b = b_ref[pl.ds(t, 1), :]
        gt = gate_ref[pl.ds(t, 1), :]
        p_t = jnp.where(sub == a, gt, 0.0).astype(BF16)
        q_t = jnp.where(sub == b, 1.0, 0.0).astype(BF16)
        row = pl.multiple_of(t * PEER_N_KEYS, PEER_N_KEYS)
        w_ref[pl.ds(row, PEER_N_KEYS), :] = _dot_nt(p_t, q_t)
        return carry

    lax.fori_loop(0, tb, token, 0, unroll=min(tb, SUB))


def _peer_wsel(a, b, gate):
    rows = a.shape[0]
    tb = _row_tile(rows, 64)
    row_spec = pl.BlockSpec((tb, PEER_HK), lambda i: (i, 0))
    w = pl.pallas_call(
        _peer_wsel_body,
        grid=(rows // tb,),
        in_specs=[row_spec, row_spec, row_spec],
        out_specs=pl.BlockSpec((tb * PEER_N_KEYS, PEER_N_KEYS), lambda i: (i, 0)),
        out_shape=jax.ShapeDtypeStruct((rows * PEER_N_KEYS, PEER_N_KEYS), F32),
        compiler_params=_cparams("parallel"),
        name="peer_wsel",
    )(a, b, gate)
    return w.reshape(rows, PEER_N_KEYS, PEER_N_KEYS)


MIX_FIRST_KEYS = 8


def _peer_mix_body(x_ref, g_ref, u_ref, v_ref, w_ref, o_ref, h_sc, acc_sc):
    j = pl.program_id(1)

    @pl.when(j == 0)
    def _():
        h_sc[...] = _rms(x_ref[...], g_ref[...], NORM_EPS).astype(BF16)
        acc_sc[...] = jnp.zeros(acc_sc.shape, F32)

    s = _dot_nt(h_sc[...], u_ref[...])
    act = 0.5 * s * (1.0 + lax.erf(s * math.sqrt(0.5)))
    z = jnp.concatenate(
        [act[:, r * PEER_N_KEYS:(r + 1) * PEER_N_KEYS] * w_ref[:, r, :]
         for r in range(MIX_FIRST_KEYS)], axis=1).astype(BF16)
    acc_sc[...] = acc_sc[...] + jnp.dot(z, v_ref[...], preferred_element_type=F32)

    @pl.when(j == pl.num_programs(1) - 1)
    def _():
        o_ref[...] = x_ref[...] + acc_sc[...]


def _peer_mix(x, g, u_bf16, v_bf16, wsel):
    rows = x.shape[0]
    n_exp = u_bf16.shape[0]
    tm = _row_tile(rows, 512)
    te = MIX_FIRST_KEYS * PEER_N_KEYS
    row_spec = pl.BlockSpec((tm, D_MODEL), lambda i, j: (i, 0))
    exp_spec = pl.BlockSpec((te, D_MODEL), lambda i, j: (j, 0))
    return pl.pallas_call(
        _peer_mix_body,
        grid=(rows // tm, n_exp // te),
        in_specs=[row_spec, pl.BlockSpec((1, D_MODEL), lambda i, j: (0, 0)),
                  exp_spec, exp_spec,
                  pl.BlockSpec((tm, MIX_FIRST_KEYS, PEER_N_KEYS), lambda i, j: (i, j, 0))],
        out_specs=row_spec,
        out_shape=jax.ShapeDtypeStruct((rows, D_MODEL), F32),
        scratch_shapes=[pltpu.VMEM((tm, D_MODEL), BF16), pltpu.VMEM((tm, D_MODEL), F32)],
        compiler_params=_cparams("parallel", "arbitrary"),
        name="peer_mix",
    )(x, g, u_bf16, v_bf16, wsel)


DEC_REP = 8
DEC_ROWS = N_HEADS_B * DEC_REP
PAGES_PER_STEP = 8


def _head_scores(q_ref, kt_refs):
    return jnp.concatenate(
        [jnp.dot(q_ref[h].astype(BF16),
                 jnp.concatenate([kt[h] for kt in kt_refs], axis=1).astype(BF16),
                 preferred_element_type=F32)
         for h in range(N_HEADS_B)], axis=0)


def _diff_decode_body(pt_ref, q_ref, knew_ref, vnew_ref, sub_ref, lq1_ref, lk1_ref, lq2_ref, lk2_ref,
                      *rest, lam_init):
    kt_refs = rest[:PAGES_PER_STEP]
    v_refs = rest[PAGES_PER_STEP:2 * PAGES_PER_STEP]
    o_ref, m_sc, l_sc, acc_sc = rest[2 * PAGES_PER_STEP:]
    step = pl.program_id(1)

    @pl.when(step == 0)
    def _():
        m_sc[...] = jnp.full(m_sc.shape, NEG_INF, F32)
        l_sc[...] = jnp.zeros(l_sc.shape, F32)
        acc_sc[...] = jnp.zeros(acc_sc.shape, F32)

    s = _head_scores(q_ref, kt_refs) * SCALE
    m_prev = m_sc[...]
    m_new = jnp.maximum(m_prev, jnp.max(s, axis=1, keepdims=True))
    alpha = jnp.exp(m_prev - m_new)
    p = jnp.exp(s - m_new)
    l_sc[...] = alpha * l_sc[...] + jnp.sum(p, axis=1, keepdims=True)
    m_sc[...] = m_new
    p = p.astype(BF16)
    pv = []
    for h in range(N_HEADS_A):
        rows = slice(2 * DEC_REP * h, 2 * DEC_REP * (h + 1))
        v_h = jnp.concatenate(
            [v_ref[pl.ds(h, PAGE_SIZE, stride=N_HEADS_A), :] for v_ref in v_refs], axis=0)
        pv.append(jnp.dot(p[rows], v_h.astype(BF16), preferred_element_type=F32))
    acc_sc[...] = alpha * acc_sc[...] + jnp.concatenate(pv, axis=0)

    @pl.when(step == pl.num_programs(1) - 1)
    def _():
        s_new = jnp.sum(q_ref[...] * knew_ref[...], axis=-1,
                        keepdims=True).reshape(DEC_ROWS, 1) * SCALE
        m_f = jnp.maximum(m_sc[...], s_new)
        a = jnp.exp(m_sc[...] - m_f)
        p_new = jnp.exp(s_new - m_f)
        l_f = a * l_sc[...] + p_new
        o = (a * acc_sc[...] + p_new * vnew_ref[...]) / l_f
        lam = (jnp.exp(jnp.sum(lq1_ref[...] * lk1_ref[...], keepdims=True))
               - jnp.exp(jnp.sum(lq2_ref[...] * lk2_ref[...], keepdims=True)) + lam_init)
        for h in range(N_HEADS_A):
            r0 = 2 * DEC_REP * h
            o_h = o[r0:r0 + DEC_REP] - lam * o[r0 + DEC_REP:r0 + 2 * DEC_REP]
            o_ref[h * DEC_REP:(h + 1) * DEC_REP, :] = (
                _rms(o_h, sub_ref[...], SUBLN_EPS) * (1.0 - lam_init))


def _diff_decode(q, k_new, v_new, cache_kt, cache_v, page_table, subln, lams, lam_init):
    batch, n_pages = page_table.shape
    steps = n_pages // PAGES_PER_STEP
    rep = lambda t: jnp.broadcast_to(t.reshape(batch, N_HEADS_B, 1, HEAD_DIM),
                                     (batch, N_HEADS_B, DEC_REP, HEAD_DIM))
    v_rep = jnp.broadcast_to(v_new.reshape(batch, N_HEADS_A, 1, DV_A),
                             (batch, N_HEADS_A, 2 * DEC_REP, DV_A)).reshape(batch, DEC_ROWS, DV_A)
    q_spec = pl.BlockSpec((None, N_HEADS_B, DEC_REP, HEAD_DIM), lambda b, s, pt: (b, 0, 0, 0))
    vec64 = pl.BlockSpec((1, HEAD_DIM), lambda b, s, pt: (0, 0))

    def page(pg):
        return lambda b, s, pt: (pt[b, s * PAGES_PER_STEP + pg], 0, 0, 0)

    def page3(pg):
        return lambda b, s, pt: (pt[b, s * PAGES_PER_STEP + pg], 0, 0)

    kt_specs = [pl.BlockSpec((None, N_HEADS_B, HEAD_DIM, PAGE_SIZE), page(pg))
                for pg in range(PAGES_PER_STEP)]
    v_specs = [pl.BlockSpec((None, PAGE_SIZE * N_HEADS_A, DV_A), page3(pg))
               for pg in range(PAGES_PER_STEP)]
    out = pl.pallas_call(
        functools.partial(_diff_decode_body, lam_init=lam_init),
        grid_spec=pltpu.PrefetchScalarGridSpec(
            num_scalar_prefetch=1,
            grid=(batch, steps),
            in_specs=[q_spec, q_spec,
                      pl.BlockSpec((None, DEC_ROWS, DV_A), lambda b, s, pt: (b, 0, 0)),
                      pl.BlockSpec((1, DV_A), lambda b, s, pt: (0, 0)),
                      vec64, vec64, vec64, vec64] + kt_specs + v_specs,
            out_specs=pl.BlockSpec((None, N_HEADS_A * DEC_REP, DV_A), lambda b, s, pt: (b, 0, 0)),
            scratch_shapes=[pltpu.VMEM((DEC_ROWS, 1), F32),
                            pltpu.VMEM((DEC_ROWS, 1), F32),
                            pltpu.VMEM((DEC_ROWS, DV_A), F32)]),
        out_shape=jax.ShapeDtypeStruct((batch, N_HEADS_A * DEC_REP, DV_A), F32),
        compiler_params=_cparams("parallel", "arbitrary"),
        name="diff_attn_decode",
    )(page_table, rep(q), rep(k_new), v_rep, subln, *lams,
      *([cache_kt] * PAGES_PER_STEP), *([cache_v] * PAGES_PER_STEP))
    return out[:, ::DEC_REP, :].reshape(batch, D_MODEL)


def _stick_decode_body(pt_ref, q_ref, *rest):
    kt_refs = rest[:PAGES_PER_STEP]
    vt_refs = rest[PAGES_PER_STEP:2 * PAGES_PER_STEP]
    o_ref, c_sc, acc_sc = rest[2 * PAGES_PER_STEP:]
    step = pl.program_id(1)

    @pl.when(step == 0)
    def _():
        c_sc[...] = jnp.zeros(c_sc.shape, F32)
        acc_sc[...] = jnp.zeros(acc_sc.shape, F32)

    tri2 = _neg_tri2(PAGE_SIZE)
    carry = c_sc[...]
    z = _head_scores(q_ref, kt_refs) * SCALE
    sp = _softplus(z)
    tails = []
    for pg in range(PAGES_PER_STEP):
        sp_pg = sp[:, pg * PAGE_SIZE:(pg + 1) * PAGE_SIZE]
        tails.append(_later_keys_tail(sp_pg, carry, tri2))
        carry = carry - jnp.sum(sp_pg, axis=1, keepdims=True)
    c_sc[...] = carry
    w = jnp.exp(z - sp + jnp.concatenate(tails, axis=1)).astype(BF16)
    acc_sc[...] = acc_sc[...] + jnp.concatenate(
        [_dot_nt(w[h * DEC_REP:(h + 1) * DEC_REP],
                 jnp.concatenate([vt[h] for vt in vt_refs], axis=1).astype(BF16))
         for h in range(N_HEADS_B)], axis=0)

    @pl.when(step == pl.num_programs(1) - 1)
    def _():
        o_ref[...] = acc_sc[...]


def _stick_decode(q, cache_kt, cache_vt, page_table):
    batch, n_pages = page_table.shape
    steps = n_pages // PAGES_PER_STEP
    q_rep = jnp.broadcast_to(q.reshape(batch, N_HEADS_B, 1, HEAD_DIM),
                             (batch, N_HEADS_B, DEC_REP, HEAD_DIM))

    def page(pg):
        return lambda b, s, pt: (pt[b, n_pages - 1 - (s * PAGES_PER_STEP + pg)], 0, 0, 0)

    specs = [pl.BlockSpec((None, N_HEADS_B, HEAD_DIM, PAGE_SIZE), page(pg))
             for pg in range(PAGES_PER_STEP)]
    out = pl.pallas_call(
        _stick_decode_body,
        grid_spec=pltpu.PrefetchScalarGridSpec(
            num_scalar_prefetch=1,
            grid=(batch, steps),
            in_specs=[pl.BlockSpec((None, N_HEADS_B, DEC_REP, HEAD_DIM),
                                   lambda b, s, pt: (b, 0, 0, 0))] + specs + specs,
            out_specs=pl.BlockSpec((None, DEC_ROWS, HEAD_DIM), lambda b, s, pt: (b, 0, 0)),
            scratch_shapes=[pltpu.VMEM((DEC_ROWS, LANES), F32),
                            pltpu.VMEM((DEC_ROWS, HEAD_DIM), F32)]),
        out_shape=jax.ShapeDtypeStruct((batch, DEC_ROWS, HEAD_DIM), F32),
        compiler_params=_cparams("parallel", "arbitrary"),
        name="stick_attn_decode",
    )(page_table, q_rep, *([cache_kt] * PAGES_PER_STEP), *([cache_vt] * PAGES_PER_STEP))
    return out[:, ::DEC_REP, :].reshape(batch, D_MODEL)


def _keys_last(cache):
    return jnp.transpose(cache, (0, 2, 3, 1))


def kernel(x_prompt, x_sample, p_prompt, p_sample, cache_k_diff, cache_v_diff, cache_k_stick, cache_v_stick, page_table, norm_mix, norm_ffn, norm_ple, norm_final, w_qkv_diff, w_o_diff, lambda_q1, lambda_k1, lambda_q2, lambda_k2, subln_diff, w_qkv_stick, w_o_stick, peer_w_q, peer_sub_keys, peer_u, peer_v, ple_w_gate, ple_w_proj):
    batch, seq, _ = x_prompt.shape
    dec_batch, dec_seq, _ = x_sample.shape
    assert dec_seq == 1
    depth = norm_mix.shape[0]
    n_pages = page_table.shape[1]
    past_len = n_pages * PAGE_SIZE
    n_p, n_s = batch * seq, dec_batch * dec_seq

    xp = x_prompt.reshape(n_p, D_MODEL)
    xs = x_sample.reshape(n_s, D_MODEL)
    cos_p, sin_p = _rope_tables(jnp.arange(seq, dtype=jnp.int32))
    cos_s, sin_s = _rope_tables(jnp.full((n_s,), past_len, jnp.int32))
    vec = lambda t: t.reshape(1, -1)
    g_final = vec(norm_final)

    new_kv = {}
    for i in range(depth):
        j = i // 2
        g_mix, g_ffn, g_ple = vec(norm_mix[i]), vec(norm_ffn[i]), vec(norm_ple[i])
        if i % 2 == 0:
            lam_init = 0.8 - 0.6 * math.exp(-0.3 * i)
            lams = [vec(lambda_q1[j]), vec(lambda_k1[j]), vec(lambda_q2[j]), vec(lambda_k2[j])]
            sub = vec(subln_diff[j])
            w_qkv = w_qkv_diff[j].astype(BF16)
            w_o = w_o_diff[j].astype(BF16)
            qp, kp, vp = _qkv(xp, g_mix, w_qkv, cos_p, sin_p, True)
            qs, kn, vn = _qkv(xs, g_mix, w_qkv, cos_s, sin_s, True)
            op = _diff_attn_prompt(qp, kp, vp, sub, lams, batch, seq, lam_init)
            os_ = _diff_decode(qs, kn, vn, _keys_last(cache_k_diff[j]),
                               cache_v_diff[j].reshape(-1, PAGE_SIZE * N_HEADS_A, DV_A),
                               page_table, sub, lams, lam_init)
            heads_k, heads_v = (2 * N_HEADS_A, HEAD_DIM), (N_HEADS_A, DV_A)
            tag = "diff"
        else:
            w_qkv = w_qkv_stick[j].astype(BF16)
            w_o = w_o_stick[j].astype(BF16)
            qp, kp, vp = _qkv(xp, g_mix, w_qkv, cos_p, sin_p, False)
            qs, kn, vn = _qkv(xs, g_mix, w_qkv, cos_s, sin_s, False)
            op = _stick_attn_prompt(qp, kp, vp, batch, seq)
            os_ = _stick_decode(qs, _keys_last(cache_k_stick[j]), _keys_last(cache_v_stick[j]),
                                page_table)
            heads_k = heads_v = (N_HEADS_B, HEAD_DIM)
            tag = "stick"
        new_kv.setdefault(tag, []).append((
            kp.reshape(batch, seq, *heads_k), vp.reshape(batch, seq, *heads_v),
            kn.reshape(dec_batch, dec_seq, *heads_k), vn.reshape(dec_batch, dec_seq, *heads_v)))
        xp = _proj_residual(op, w_o, xp)
        xs = _proj_residual(os_, w_o, xs)

        wq = peer_w_q[i].astype(BF16)
        sk = peer_sub_keys[i].reshape(2 * PEER_HEADS, PEER_N_KEYS, PEER_HALF).astype(BF16)
        u, v = peer_u[i].astype(BF16), peer_v[i].astype(BF16)
        xp = _peer_mix(xp, g_ffn, u, v, _peer_wsel(*_peer_route(xp, g_ffn, wq, sk)))
        xs = _peer_mix(xs, g_ffn, u, v, _peer_wsel(*_peer_route(xs, g_ffn, wq, sk)))

        wg, wp = ple_w_gate[i].astype(BF16), ple_w_proj[i].astype(BF16)
        last = i == depth - 1
        xp = _ple(xp, p_prompt[i].reshape(n_p, -1), g_ple, wg, wp, g_final, last)
        xs = _ple(xs, p_sample[i].reshape(n_s, -1), g_ple, wg, wp, g_final, last)

    stack = lambda tag, idx: jnp.stack([t[idx] for t in new_kv[tag]])
    return (xp.reshape(batch, seq, D_MODEL), xs.reshape(dec_batch, dec_seq, D_MODEL),
            stack("diff", 0), stack("diff", 1), stack("stick", 0), stack("stick", 1),
            stack("diff", 2), stack("diff", 3), stack("stick", 2), stack("stick", 3))
```

```python
import functools
import math

import jax
import jax.numpy as jnp
import numpy as np
from jax import lax
from jax.experimental import pallas as pl
from jax.experimental.pallas import tpu as pltpu

F32 = jnp.float32
BF16 = jnp.bfloat16

D_MODEL = 1024
HEAD_DIM = 64
N_HEADS_A = 8
N_HEADS_B = 16
DV_A = 2 * HEAD_DIM
ROPE_THETA = 10000.0
NORM_EPS = 1e-6
SUBLN_EPS = 1e-5
PAGE_SIZE = 128
PEER_HEADS = 8
PEER_N_KEYS = 128
PEER_TOPK = 16
PEER_HALF = 128
PEER_HK = PEER_HEADS * PEER_TOPK
SCALE = HEAD_DIM ** -0.5
LOG2_E = math.log2(math.e)

LANES = 128
VMEM_LIMIT = 56 * 1024 * 1024
NEG_INF = float("-inf")


def _cparams(*sem):
    return pltpu.CompilerParams(dimension_semantics=sem, vmem_limit_bytes=VMEM_LIMIT)


def _rms(x, g, eps):
    return x * lax.rsqrt(jnp.mean(x * x, axis=-1, keepdims=True) + eps) * g


def _dot_nt(a, b):
    return lax.dot_general(a, b, (((1,), (1,)), ((), ())), preferred_element_type=F32)


def _row_tile(rows, cap):
    return rows if rows <= cap else cap


def _qkv_body(x_ref, g_ref, w_ref, cos_ref, sin_ref, q_ref, k_ref, v_ref, *, rope, k_t, v_t):
    h = _rms(x_ref[...], g_ref[...], NORM_EPS).astype(BF16)
    y = jnp.dot(h, w_ref[...], preferred_element_type=F32)
    q, k, v = y[:, :D_MODEL], y[:, D_MODEL:2 * D_MODEL], y[:, 2 * D_MODEL:]
    if rope:
        cos = jnp.tile(cos_ref[...], (1, D_MODEL // LANES))
        sin = jnp.tile(sin_ref[...], (1, D_MODEL // LANES))
        lane = lax.broadcasted_iota(jnp.int32, q.shape, 1)
        first_half = (lane % HEAD_DIM) < (HEAD_DIM // 2)

        def rot(t):
            partner = jnp.where(first_half,
                                pltpu.roll(t, D_MODEL - HEAD_DIM // 2, 1),
                                pltpu.roll(t, HEAD_DIM // 2, 1))
            return t * cos + partner * sin

        q, k = rot(q), rot(k)
    q_ref[...] = q
    k_ref[...] = k.T if k_t else k
    v_ref[...] = v.T if v_t else v


def _qkv(x, g, w_bf16, cos, sin, rope, seq=None, k_t=False, v_t=False):
    rows = x.shape[0]
    tm = _row_tile(rows, 256)
    n_tab = cos.shape[0] // tm
    row_spec = pl.BlockSpec((tm, D_MODEL), lambda i: (i, 0))
    tab_spec = pl.BlockSpec((tm, LANES), lambda i: (i % n_tab, 0))
    out = jax.ShapeDtypeStruct((rows, D_MODEL), F32)
    if k_t or v_t:
        per_seq = seq // tm
        t_spec = pl.BlockSpec((None, D_MODEL, tm), lambda i: (i // per_seq, 0, i % per_seq))
        t_out = jax.ShapeDtypeStruct((rows // seq, D_MODEL, seq), F32)
    return pl.pallas_call(
        functools.partial(_qkv_body, rope=rope, k_t=k_t, v_t=v_t),
        grid=(rows // tm,),
        in_specs=[row_spec,
                  pl.BlockSpec((1, D_MODEL), lambda i: (0, 0)),
                  pl.BlockSpec((D_MODEL, 3 * D_MODEL), lambda i: (0, 0)),
                  tab_spec, tab_spec],
        out_specs=[row_spec, t_spec if k_t else row_spec, t_spec if v_t else row_spec],
        out_shape=[out, t_out if k_t else out, t_out if v_t else out],
        compiler_params=_cparams("parallel"),
        name="qkv_proj",
    )(x, g, w_bf16, cos, sin)


def _rope_tables(pos):
    half = HEAD_DIM // 2
    inv = 1.0 / (ROPE_THETA ** (jnp.arange(half, dtype=F32) * (2.0 / HEAD_DIM)))
    ang = pos.astype(F32)[:, None] * inv[None, :]
    cos, sin = jnp.cos(ang), jnp.sin(ang)
    cos_t = jnp.concatenate([cos, cos, cos, cos], axis=1)
    sin_t = jnp.concatenate([-sin, sin, -sin, sin], axis=1)
    return cos_t, sin_t


def _proj_body(a_ref, w_ref, r_ref, o_ref):
    o_ref[...] = r_ref[...] + jnp.dot(a_ref[...].astype(BF16), w_ref[...],
                                      preferred_element_type=F32)


def _proj_residual(a, w_bf16, res):
    rows = a.shape[0]
    tm = _row_tile(rows, 512)
    row_spec = pl.BlockSpec((tm, D_MODEL), lambda i: (i, 0))
    return pl.pallas_call(
        _proj_body,
        grid=(rows // tm,),
        in_specs=[row_spec, pl.BlockSpec((D_MODEL, D_MODEL), lambda i: (0, 0)), row_spec],
        out_specs=row_spec,
        out_shape=jax.ShapeDtypeStruct((rows, D_MODEL), F32),
        compiler_params=_cparams("parallel"),
        name="out_proj",
    )(a, w_bf16, res)


def _ple_body(x_ref, p_ref, g_ref, wg_ref, wp_ref, gf_ref, o_ref, *, final_norm):
    x = x_ref[...]
    h = _rms(x, g_ref[...], NORM_EPS).astype(BF16)
    gate = jax.nn.sigmoid(jnp.dot(h, wg_ref[...], preferred_element_type=F32))
    proj = jnp.dot(p_ref[...].astype(BF16), wp_ref[...], preferred_element_type=F32)
    y = x + gate * proj
    if final_norm:
        y = _rms(y, gf_ref[...], NORM_EPS)
    o_ref[...] = y


def _ple(x, p, g, wg_bf16, wp_bf16, g_final, final_norm):
    rows = x.shape[0]
    ple_dim = p.shape[1]
    tm = _row_tile(rows, 512)
    row_spec = pl.BlockSpec((tm, D_MODEL), lambda i: (i, 0))
    vec_spec = pl.BlockSpec((1, D_MODEL), lambda i: (0, 0))
    return pl.pallas_call(
        functools.partial(_ple_body, final_norm=final_norm),
        grid=(rows // tm,),
        in_specs=[row_spec, pl.BlockSpec((tm, ple_dim), lambda i: (i, 0)), vec_spec,
                  pl.BlockSpec((D_MODEL, D_MODEL), lambda i: (0, 0)),
                  pl.BlockSpec((ple_dim, D_MODEL), lambda i: (0, 0)), vec_spec],
        out_specs=row_spec,
        out_shape=jax.ShapeDtypeStruct((rows, D_MODEL), F32),
        compiler_params=_cparams("parallel"),
        name="ple",
    )(x, p, g, wg_bf16, wp_bf16, g_final)


def _stack_heads(q):
    lane = lax.broadcasted_iota(jnp.int32, q.shape, 1)
    q0 = jnp.where(lane < HEAD_DIM, q, 0.0)
    q1 = jnp.where(lane >= HEAD_DIM, q, 0.0)
    return jnp.concatenate([q0, q1], axis=0)


def _causal_steps(nq, descending):
    qi, kj = [], []
    for i in range(nq):
        for j in (range(i, -1, -1) if descending else range(i + 1)):
            qi.append(i)
            kj.append(j)
    return np.asarray(qi, np.int32), np.asarray(kj, np.int32)


def _diag_mask(tq, strict, stacked=1):
    row = lax.broadcasted_iota(jnp.int32, (stacked * tq, tq), 0) & (tq - 1)
    col = lax.broadcasted_iota(jnp.int32, (stacked * tq, tq), 1)
    return col < row if strict else col <= row


def _across(stat, width):
    return jnp.tile(stat, (1, width // LANES))


def _attn_tile(seq, cap):
    tq = min(seq, cap)
    assert seq % tq == 0 and tq & (tq - 1) == 0 and tq % LANES == 0
    return tq


def _diff_attn_body(qi_ref, kj_ref, q_ref, kt_ref, v_ref, sub_ref, lq1_ref, lk1_ref, lq2_ref,
                    lk2_ref, o_ref, qs_sc, m_sc, l_sc, acc_sc, *, tq, lam_init):
    t = pl.program_id(2)
    i, j = qi_ref[t], kj_ref[t]

    @pl.when(j == 0)
    def _():
        qs_sc[...] = _stack_heads(q_ref[...] * (SCALE * LOG2_E)).astype(BF16)
        m_sc[...] = jnp.full(m_sc.shape, NEG_INF, F32)
        l_sc[...] = jnp.zeros(l_sc.shape, F32)
        acc_sc[...] = jnp.zeros(acc_sc.shape, F32)

    def update(diagonal):
        s = jnp.dot(qs_sc[...], kt_ref[...].astype(BF16), preferred_element_type=F32)
        if diagonal:
            s = jnp.where(_diag_mask(tq, strict=False, stacked=2), s, NEG_INF)
        m_prev = m_sc[...]
        m_new = jnp.maximum(m_prev, jnp.max(s, axis=1, keepdims=True))
        alpha = jnp.exp2(m_prev - m_new)
        p = jnp.exp2(s - _across(m_new, tq))
        l_sc[...] = alpha * l_sc[...] + jnp.sum(p, axis=1, keepdims=True)
        acc_sc[...] = alpha * acc_sc[...] + jnp.dot(
            p.astype(BF16), v_ref[...].astype(BF16), preferred_element_type=F32)
        m_sc[...] = m_new

    @pl.when(j < i)
    def _():
        update(False)

    @pl.when(j == i)
    def _():
        update(True)
        lam = (jnp.exp(jnp.sum(lq1_ref[...] * lk1_ref[...], keepdims=True))
               - jnp.exp(jnp.sum(lq2_ref[...] * lk2_ref[...], keepdims=True)) + lam_init)
        a = acc_sc[...] / l_sc[...]
        o = a[:tq] - lam * a[tq:]
        o_ref[...] = _rms(o, sub_ref[...], SUBLN_EPS) * (1.0 - lam_init)


def _diff_attn_prompt(q, k_t, v, subln, lams, batch, seq, lam_init):
    tq = _attn_tile(seq, 512)
    nq = seq // tq
    qi, kj = _causal_steps(nq, descending=False)
    const = lambda shape: pl.BlockSpec(shape, lambda b, h, t, qi, kj: (0, 0))
    q_spec = pl.BlockSpec((tq, LANES), lambda b, h, t, qi, kj: (b * nq + qi[t], h))
    kt_spec = pl.BlockSpec((None, LANES, tq), lambda b, h, t, qi, kj: (b, h, kj[t]))
    v_spec = pl.BlockSpec((tq, LANES), lambda b, h, t, qi, kj: (b * nq + kj[t], h))
    vec64 = const((1, HEAD_DIM))
    return pl.pallas_call(
        functools.partial(_diff_attn_body, tq=tq, lam_init=lam_init),
        grid_spec=pltpu.PrefetchScalarGridSpec(
            num_scalar_prefetch=2,
            grid=(batch, N_HEADS_A, len(qi)),
            in_specs=[q_spec, kt_spec, v_spec, const((1, DV_A)), vec64, vec64, vec64, vec64],
            out_specs=q_spec,
            scratch_shapes=[pltpu.VMEM((2 * tq, LANES), BF16),
                            pltpu.VMEM((2 * tq, LANES), F32),
                            pltpu.VMEM((2 * tq, LANES), F32),
                            pltpu.VMEM((2 * tq, LANES), F32)]),
        out_shape=jax.ShapeDtypeStruct((batch * seq, D_MODEL), F32),
        compiler_params=_cparams("parallel", "parallel", "arbitrary"),
        name="diff_attn_prompt",
    )(qi, kj, q, k_t, v, subln, *lams)


def _softplus(z):
    return jnp.maximum(z, 0.0) + jnp.log(1.0 + jnp.exp(jnp.minimum(z, -z)))


def _softplus2(z2):
    return jnp.maximum(z2, 0.0) + jnp.log2(1.0 + jnp.exp2(jnp.minimum(z2, -z2)))


def _split_bf16(x):
    hi = x.astype(BF16)
    lo = (x - hi.astype(F32)).astype(BF16)
    return hi, lo


def _neg_tri2(n):
    r = lax.broadcasted_iota(jnp.int32, (2 * n, n), 0) & (n - 1)
    c = lax.broadcasted_iota(jnp.int32, (2 * n, n), 1)
    return jnp.where(r > c, -1.0, 0.0).astype(BF16)


def _later_keys_tail(cost, carry, tri2):
    hi, lo = _split_bf16(cost)
    later = jnp.dot(jnp.concatenate([hi, lo], axis=1), tri2, preferred_element_type=F32)
    return later + _across(carry, cost.shape[1])


STICK_SUB = 256


def _stick_attn_body(qi_ref, kj_ref, q_ref, kt_ref, vt_ref, o_ref, qs_sc, c_sc, acc_sc, tri_sc, *, tq):
    t = pl.program_id(2)
    i, j = qi_ref[t], kj_ref[t]
    sub = tri_sc.shape[1]

    @pl.when(j == i)
    def _():
        qs_sc[...] = _stack_heads(q_ref[...] * (SCALE * LOG2_E)).astype(BF16)
        c_sc[...] = jnp.zeros(c_sc.shape, F32)
        acc_sc[...] = jnp.zeros(acc_sc.shape, F32)
        tri_sc[...] = _neg_tri2(sub)

    def update(diagonal):
        z = jnp.dot(qs_sc[...], kt_ref[...].astype(BF16), preferred_element_type=F32)
        sp = _softplus2(z)
        if diagonal:
            mask = _diag_mask(tq, strict=True, stacked=2)
            cost = jnp.where(mask, sp, 0.0)
        else:
            cost = sp
        carry = c_sc[...]
        tails = []
        for blk in reversed(range(tq // sub)):
            cost_b = cost[:, blk * sub:(blk + 1) * sub]
            tails.append(_later_keys_tail(cost_b, carry, tri_sc[...]))
            carry = carry - jnp.sum(cost_b, axis=1, keepdims=True)
        c_sc[...] = carry
        w = jnp.exp2(z - sp + jnp.concatenate(tails[::-1], axis=1))
        if diagonal:
            w = jnp.where(mask, w, 0.0)
        acc_sc[...] = acc_sc[...] + _dot_nt(w.astype(BF16), vt_ref[...].astype(BF16))

    @pl.when(j < i)
    def _():
        update(False)

    @pl.when(j == i)
    def _():
        update(True)

    @pl.when(j == 0)
    def _():
        a = acc_sc[...]
        lane = lax.broadcasted_iota(jnp.int32, (tq, LANES), 1)
        o_ref[...] = jnp.where(lane < HEAD_DIM, a[:tq], a[tq:])


def _stick_attn_prompt(q, k_t, v_t, batch, seq):
    tq = _attn_tile(seq, 512)
    sub = min(tq, STICK_SUB)
    nq = seq // tq
    qi, kj = _causal_steps(nq, descending=True)
    q_spec = pl.BlockSpec((tq, LANES), lambda b, h, t, qi, kj: (b * nq + qi[t], h))
    kv_spec = pl.BlockSpec((None, LANES, tq), lambda b, h, t, qi, kj: (b, h, kj[t]))
    return pl.pallas_call(
        functools.partial(_stick_attn_body, tq=tq),
        grid_spec=pltpu.PrefetchScalarGridSpec(
            num_scalar_prefetch=2,
            grid=(batch, N_HEADS_B // 2, len(qi)),
            in_specs=[q_spec, kv_spec, kv_spec],
            out_specs=q_spec,
            scratch_shapes=[pltpu.VMEM((2 * tq, LANES), BF16),
                            pltpu.VMEM((2 * tq, LANES), F32),
                            pltpu.VMEM((2 * tq, LANES), F32),
                            pltpu.VMEM((2 * sub, sub), BF16)]),
        out_shape=jax.ShapeDtypeStruct((batch * seq, D_MODEL), F32),
        compiler_params=_cparams("parallel", "parallel", "arbitrary"),
        name="stick_attn_prompt",
    )(qi, kj, q, k_t, v_t)


def _topk_axis0(s, order, payloads, k):
    big = jnp.iinfo(jnp.int32).max if order.dtype == jnp.int32 else jnp.inf
    vals, wins, picked = [], [], [[] for _ in payloads]
    for _ in range(k):
        m = jnp.max(s, axis=0, keepdims=True)
        win = jnp.min(jnp.where(s == m, order, big), axis=0, keepdims=True)
        hit = order == win
        vals.append(m)
        wins.append(win)
        for out, pay in zip(picked, payloads):
            out.append(jnp.sum(jnp.where(hit, pay, 0.0), axis=0, keepdims=True))
        s = jnp.where(hit, NEG_INF, s)
    cat = lambda rows: jnp.concatenate(rows, axis=0)
    return cat(vals), cat(wins), [cat(p) for p in picked]


SUB = 8


def _pair_candidates():
    pairs = [(0, j) for j in range(SUB)] + [(0, j) for j in range(SUB, 2 * SUB)]
    for i in range(1, SUB):
        pairs += [(i, j) for j in range(SUB)]
    pairs += [(i, 0) for i in range(SUB, 2 * SUB)]
    flat = np.asarray([i * PEER_TOPK + j for i, j in pairs], np.float32)
    bias = np.asarray([0.0 if (i + 1) * (j + 1) <= PEER_TOPK else NEG_INF for i, j in pairs],
                      np.float32)
    return flat, bias


def _by_first_rank(x):
    t = x.shape[1]
    rep = lambda i: jnp.broadcast_to(x[i:i + 1], (SUB, t))
    return jnp.concatenate([rep(0)] + [rep(i) for i in range(SUB)] + [x[SUB:]], axis=0)


def _by_second_rank(y):
    t = y.shape[1]
    return jnp.concatenate([y[:SUB], y[SUB:]] + [y[:SUB]] * (SUB - 1)
                           + [jnp.broadcast_to(y[0:1], (SUB, t))], axis=0)


def _peer_route_body(x_ref, g_ref, wq_ref, sk_ref, flat_ref, bias_ref, a_ref, b_ref, gate_ref,
                     q_sc, a_sc, b_sc, gate_sc):
    tm = x_ref.shape[0]
    h = _rms(x_ref[...], g_ref[...], NORM_EPS).astype(BF16)
    q_sc[...] = jnp.dot(h, wq_ref[...], preferred_element_type=F32).astype(BF16)
    key_id = lax.broadcasted_iota(jnp.int32, (PEER_N_KEYS, tm), 0)

    def head(hd, carry):
        tops = []
        for c in range(2):
            col = pl.multiple_of((hd * 2 + c) * PEER_HALF, PEER_HALF)
            s_t = _dot_nt(sk_ref[hd * 2 + c], q_sc[:, pl.ds(col, PEER_HALF)])
            vals, ids, _ = _topk_axis0(s_t, key_id, [], PEER_TOPK)
            tops.append((vals, ids.astype(F32)))
        (s1, i1), (s2, i2) = tops
        cand = _by_first_rank(s1) + _by_second_rank(s2) + bias_ref[...]
        top_s, _, (ea, eb) = _topk_axis0(cand, flat_ref[...],
                                         [_by_first_rank(i1), _by_second_rank(i2)], PEER_TOPK)
        e = jnp.exp(top_s - jnp.max(top_s, axis=0, keepdims=True))
        gate = e / jnp.sum(e, axis=0, keepdims=True)
        row = pl.multiple_of(hd * PEER_TOPK, PEER_TOPK)
        a_sc[pl.ds(row, PEER_TOPK), :] = ea
        b_sc[pl.ds(row, PEER_TOPK), :] = eb
        gate_sc[pl.ds(row, PEER_TOPK), :] = gate
        return carry

    lax.fori_loop(0, PEER_HEADS, head, 0, unroll=4)
    a_ref[...] = a_sc[...].T
    b_ref[...] = b_sc[...].T
    gate_ref[...] = gate_sc[...].T


def _peer_route(x, g, wq_bf16, sk_bf16):
    rows = x.shape[0]
    tm = _row_tile(rows, 256)
    qdim = wq_bf16.shape[1]
    row_spec = pl.BlockSpec((tm, PEER_HK), lambda i: (i, 0))
    out = jax.ShapeDtypeStruct((rows, PEER_HK), F32)
    flat, bias = (np.ascontiguousarray(np.broadcast_to(t[:, None], (t.shape[0], tm)))
                  for t in _pair_candidates())
    tab_spec = pl.BlockSpec(flat.shape, lambda i: (0, 0))
    return pl.pallas_call(
        _peer_route_body,
        grid=(rows // tm,),
        in_specs=[pl.BlockSpec((tm, D_MODEL), lambda i: (i, 0)),
                  pl.BlockSpec((1, D_MODEL), lambda i: (0, 0)),
                  pl.BlockSpec((D_MODEL, qdim), lambda i: (0, 0)),
                  pl.BlockSpec(sk_bf16.shape, lambda i: (0, 0, 0)),
                  tab_spec, tab_spec],
        out_specs=[row_spec, row_spec, row_spec],
        out_shape=[out, out, out],
        scratch_shapes=[pltpu.VMEM((tm, qdim), BF16),
                        pltpu.VMEM((PEER_HK, tm), F32),
                        pltpu.VMEM((PEER_HK, tm), F32),
                        pltpu.VMEM((PEER_HK, tm), F32)],
        compiler_params=_cparams("parallel"),
        name="peer_route",
    )(x, g, wq_bf16, sk_bf16, flat, bias)


def _peer_wsel_body(a_ref, b_ref, gate_ref, w_ref):
    tb = a_ref.shape[0]
    sub = lax.broadcasted_iota(jnp.int32, (PEER_N_KEYS, PEER_HK), 0).astype(F32)

    def token(t, carry):
        a = a_ref[pl.ds(t, 1), :]
        b = b_ref[pl.ds(t, 1), :]
        gt = gate_ref[pl.ds(t, 1), :]
        p_t = jnp.where(sub == a, gt, 0.0).astype(BF16)
        q_t = jnp.where(sub == b, 1.0, 0.0).astype(BF16)
        row = pl.multiple_of(t * PEER_N_KEYS, PEER_N_KEYS)
        w_ref[pl.ds(row, PEER_N_KEYS), :] = _dot_nt(p_t, q_t)
        return carry

    lax.fori_loop(0, tb, token, 0, unroll=True)


def _peer_wsel(a, b, gate):
    rows = a.shape[0]
    tb = _row_tile(rows, 64)
    row_spec = pl.BlockSpec((tb, PEER_HK), lambda i: (i, 0))
    w = pl.pallas_call(
        _peer_wsel_body,
        grid=(rows // tb,),
        in_specs=[row_spec, row_spec, row_spec],
        out_specs=pl.BlockSpec((tb * PEER_N_KEYS, PEER_N_KEYS), lambda i: (i, 0)),
        out_shape=jax.ShapeDtypeStruct((rows * PEER_N_KEYS, PEER_N_KEYS), F32),
        compiler_params=_cparams("parallel"),
        name="peer_wsel",
    )(a, b, gate)
    return w.reshape(rows, PEER_N_KEYS, PEER_N_KEYS)


MIX_FIRST_KEYS = 8


def _peer_mix_body(x_ref, g_ref, u_ref, v_ref, w_ref, o_ref, h_sc, acc_sc):
    j = pl.program_id(1)

    @pl.when(j == 0)
    def _():
        h_sc[...] = _rms(x_ref[...], g_ref[...], NORM_EPS).astype(BF16)
        acc_sc[...] = jnp.zeros(acc_sc.shape, F32)

    s = _dot_nt(h_sc[...], u_ref[...])
    act = 0.5 * s * (1.0 + lax.erf(s * math.sqrt(0.5)))
    z = jnp.concatenate(
        [act[:, r * PEER_N_KEYS:(r + 1) * PEER_N_KEYS] * w_ref[:, r, :]
         for r in range(MIX_FIRST_KEYS)], axis=1).astype(BF16)
    acc_sc[...] = acc_sc[...] + jnp.dot(z, v_ref[...], preferred_element_type=F32)

    @pl.when(j == pl.num_programs(1) - 1)
    def _():
        o_ref[...] = x_ref[...] + acc_sc[...]


def _peer_mix(x, g, u_bf16, v_bf16, wsel):
    rows = x.shape[0]
    n_exp = u_bf16.shape[0]
    tm = _row_tile(rows, 512)
    te = MIX_FIRST_KEYS * PEER_N_KEYS
    row_spec = pl.BlockSpec((tm, D_MODEL), lambda i, j: (i, 0))
    exp_spec = pl.BlockSpec((te, D_MODEL), lambda i, j: (j, 0))
    return pl.pallas_call(
        _peer_mix_body,
        grid=(rows // tm, n_exp // te),
        in_specs=[row_spec, pl.BlockSpec((1, D_MODEL), lambda i, j: (0, 0)),
                  exp_spec, exp_spec,
                  pl.BlockSpec((tm, MIX_FIRST_KEYS, PEER_N_KEYS), lambda i, j: (i, j, 0))],
        out_specs=row_spec,
        out_shape=jax.ShapeDtypeStruct((rows, D_MODEL), F32),
        scratch_shapes=[pltpu.VMEM((tm, D_MODEL), BF16), pltpu.VMEM((tm, D_MODEL), F32)],
        compiler_params=_cparams("parallel", "arbitrary"),
        name="peer_mix",
    )(x, g, u_bf16, v_bf16, wsel)


DEC_REP = 8
DEC_ROWS = N_HEADS_B * DEC_REP
PAGES_PER_STEP = 8


def _head_scores(q_ref, kt_refs):
    return jnp.concatenate(
        [jnp.dot(q_ref[h].astype(BF16),
                 jnp.concatenate([kt[h] for kt in kt_refs], axis=1).astype(BF16),
                 preferred_element_type=F32)
         for h in range(N_HEADS_B)], axis=0)


def _diff_decode_body(pt_ref, q_ref, knew_ref, vnew_ref, sub_ref, lq1_ref, lk1_ref, lq2_ref, lk2_ref,
                      *rest, lam_init):
    kt_refs = rest[:PAGES_PER_STEP]
    v_refs = rest[PAGES_PER_STEP:2 * PAGES_PER_STEP]
    o_ref, m_sc, l_sc, acc_sc = rest[2 * PAGES_PER_STEP:]
    step = pl.program_id(1)

    @pl.when(step == 0)
    def _():
        m_sc[...] = jnp.full(m_sc.shape, NEG_INF, F32)
        l_sc[...] = jnp.zeros(l_sc.shape, F32)
        acc_sc[...] = jnp.zeros(acc_sc.shape, F32)

    s = _head_scores(q_ref, kt_refs) * SCALE
    m_prev = m_sc[...]
    m_new = jnp.maximum(m_prev, jnp.max(s, axis=1, keepdims=True))
    alpha = jnp.exp(m_prev - m_new)
    p = jnp.exp(s - m_new)
    l_sc[...] = alpha * l_sc[...] + jnp.sum(p, axis=1, keepdims=True)
    m_sc[...] = m_new
    p = p.astype(BF16)
    pv = []
    for h in range(N_HEADS_A):
        rows = slice(2 * DEC_REP * h, 2 * DEC_REP * (h + 1))
        v_h = jnp.concatenate(
            [v_ref[pl.ds(h, PAGE_SIZE, stride=N_HEADS_A), :] for v_ref in v_refs], axis=0)
        pv.append(jnp.dot(p[rows], v_h.astype(BF16), preferred_element_type=F32))
    acc_sc[...] = alpha * acc_sc[...] + jnp.concatenate(pv, axis=0)

    @pl.when(step == pl.num_programs(1) - 1)
    def _():
        s_new = jnp.sum(q_ref[...] * knew_ref[...], axis=-1,
                        keepdims=True).reshape(DEC_ROWS, 1) * SCALE
        m_f = jnp.maximum(m_sc[...], s_new)
        a = jnp.exp(m_sc[...] - m_f)
        p_new = jnp.exp(s_new - m_f)
        l_f = a * l_sc[...] + p_new
        o = (a * acc_sc[...] + p_new * vnew_ref[...]) / l_f
        lam = (jnp.exp(jnp.sum(lq1_ref[...] * lk1_ref[...], keepdims=True))
               - jnp.exp(jnp.sum(lq2_ref[...] * lk2_ref[...], keepdims=True)) + lam_init)
        for h in range(N_HEADS_A):
            r0 = 2 * DEC_REP * h
            o_h = o[r0:r0 + DEC_REP] - lam * o[r0 + DEC_REP:r0 + 2 * DEC_REP]
            o_ref[h * DEC_REP:(h + 1) * DEC_REP, :] = (
                _rms(o_h, sub_ref[...], SUBLN_EPS) * (1.0 - lam_init))


def _diff_decode(q, k_new, v_new, cache_kt, cache_v, page_table, subln, lams, lam_init):
    batch, n_pages = page_table.shape
    steps = n_pages // PAGES_PER_STEP
    rep = lambda t: jnp.broadcast_to(t.reshape(batch, N_HEADS_B, 1, HEAD_DIM),
                                     (batch, N_HEADS_B, DEC_REP, HEAD_DIM))
    v_rep = jnp.broadcast_to(v_new.reshape(batch, N_HEADS_A, 1, DV_A),
                             (batch, N_HEADS_A, 2 * DEC_REP, DV_A)).reshape(batch, DEC_ROWS, DV_A)
    q_spec = pl.BlockSpec((None, N_HEADS_B, DEC_REP, HEAD_DIM), lambda b, s, pt: (b, 0, 0, 0))
    vec64 = pl.BlockSpec((1, HEAD_DIM), lambda b, s, pt: (0, 0))

    def page(pg):
        return lambda b, s, pt: (pt[b, s * PAGES_PER_STEP + pg], 0, 0, 0)

    def page3(pg):
        return lambda b, s, pt: (pt[b, s * PAGES_PER_STEP + pg], 0, 0)

    kt_specs = [pl.BlockSpec((None, N_HEADS_B, HEAD_DIM, PAGE_SIZE), page(pg))
                for pg in range(PAGES_PER_STEP)]
    v_specs = [pl.BlockSpec((None, PAGE_SIZE * N_HEADS_A, DV_A), page3(pg))
               for pg in range(PAGES_PER_STEP)]
    out = pl.pallas_call(
        functools.partial(_diff_decode_body, lam_init=lam_init),
        grid_spec=pltpu.PrefetchScalarGridSpec(
            num_scalar_prefetch=1,
            grid=(batch, steps),
            in_specs=[q_spec, q_spec,
                      pl.BlockSpec((None, DEC_ROWS, DV_A), lambda b, s, pt: (b, 0, 0)),
                      pl.BlockSpec((1, DV_A), lambda b, s, pt: (0, 0)),
                      vec64, vec64, vec64, vec64] + kt_specs + v_specs,
            out_specs=pl.BlockSpec((None, N_HEADS_A * DEC_REP, DV_A), lambda b, s, pt: (b, 0, 0)),
            scratch_shapes=[pltpu.VMEM((DEC_ROWS, 1), F32),
                            pltpu.VMEM((DEC_ROWS, 1), F32),
                            pltpu.VMEM((DEC_ROWS, DV_A), F32)]),
        out_shape=jax.ShapeDtypeStruct((batch, N_HEADS_A * DEC_REP, DV_A), F32),
        compiler_params=_cparams("parallel", "arbitrary"),
        name="diff_attn_decode",
    )(page_table, rep(q), rep(k_new), v_rep, subln, *lams,
      *([cache_kt] * PAGES_PER_STEP), *([cache_v] * PAGES_PER_STEP))
    return out[:, ::DEC_REP, :].reshape(batch, D_MODEL)


def _stick_decode_body(pt_ref, q_ref, *rest):
    kt_refs = rest[:PAGES_PER_STEP]
    vt_refs = rest[PAGES_PER_STEP:2 * PAGES_PER_STEP]
    o_ref, c_sc, acc_sc = rest[2 * PAGES_PER_STEP:]
    step = pl.program_id(1)

    @pl.when(step == 0)
    def _():
        c_sc[...] = jnp.zeros(c_sc.shape, F32)
        acc_sc[...] = jnp.zeros(acc_sc.shape, F32)

    tri2 = _neg_tri2(PAGE_SIZE)
    carry = c_sc[...]
    z = _head_scores(q_ref, kt_refs) * SCALE
    sp = _softplus(z)
    tails = []
    for pg in range(PAGES_PER_STEP):
        sp_pg = sp[:, pg * PAGE_SIZE:(pg + 1) * PAGE_SIZE]
        tails.append(_later_keys_tail(sp_pg, carry, tri2))
        carry = carry - jnp.sum(sp_pg, axis=1, keepdims=True)
    c_sc[...] = carry
    w = jnp.exp(z - sp + jnp.concatenate(tails, axis=1)).astype(BF16)
    acc_sc[...] = acc_sc[...] + jnp.concatenate(
        [_dot_nt(w[h * DEC_REP:(h + 1) * DEC_REP],
                 jnp.concatenate([vt[h] for vt in vt_refs], axis=1).astype(BF16))
         for h in range(N_HEADS_B)], axis=0)

    @pl.when(step == pl.num_programs(1) - 1)
    def _():
        o_ref[...] = acc_sc[...]


def _stick_decode(q, cache_kt, cache_vt, page_table):
    batch, n_pages = page_table.shape
    steps = n_pages // PAGES_PER_STEP
    q_rep = jnp.broadcast_to(q.reshape(batch, N_HEADS_B, 1, HEAD_DIM),
                             (batch, N_HEADS_B, DEC_REP, HEAD_DIM))

    def page(pg):
        return lambda b, s, pt: (pt[b, n_pages - 1 - (s * PAGES_PER_STEP + pg)], 0, 0, 0)

    specs = [pl.BlockSpec((None, N_HEADS_B, HEAD_DIM, PAGE_SIZE), page(pg))
             for pg in range(PAGES_PER_STEP)]
    out = pl.pallas_call(
        _stick_decode_body,
        grid_spec=pltpu.PrefetchScalarGridSpec(
            num_scalar_prefetch=1,
            grid=(batch, steps),
            in_specs=[pl.BlockSpec((None, N_HEADS_B, DEC_REP, HEAD_DIM),
                                   lambda b, s, pt: (b, 0, 0, 0))] + specs + specs,
            out_specs=pl.BlockSpec((None, DEC_ROWS, HEAD_DIM), lambda b, s, pt: (b, 0, 0)),
            scratch_shapes=[pltpu.VMEM((DEC_ROWS, LANES), F32),
                            pltpu.VMEM((DEC_ROWS, HEAD_DIM), F32)]),
        out_shape=jax.ShapeDtypeStruct((batch, DEC_ROWS, HEAD_DIM), F32),
        compiler_params=_cparams("parallel", "arbitrary"),
        name="stick_attn_decode",
    )(page_table, q_rep, *([cache_kt] * PAGES_PER_STEP), *([cache_vt] * PAGES_PER_STEP))
    return out[:, ::DEC_REP, :].reshape(batch, D_MODEL)


def _keys_last(cache):
    return jnp.transpose(cache, (0, 2, 3, 1))


def kernel(x_prompt, x_sample, p_prompt, p_sample, cache_k_diff, cache_v_diff, cache_k_stick, cache_v_stick, page_table, norm_mix, norm_ffn, norm_ple, norm_final, w_qkv_diff, w_o_diff, lambda_q1, lambda_k1, lambda_q2, lambda_k2, subln_diff, w_qkv_stick, w_o_stick, peer_w_q, peer_sub_keys, peer_u, peer_v, ple_w_gate, ple_w_proj):
    batch, seq, _ = x_prompt.shape
    dec_batch, dec_seq, _ = x_sample.shape
    assert dec_seq == 1
    depth = norm_mix.shape[0]
    n_pages = page_table.shape[1]
    past_len = n_pages * PAGE_SIZE
    n_p, n_s = batch * seq, dec_batch * dec_seq

    xp = x_prompt.reshape(n_p, D_MODEL)
    xs = x_sample.reshape(n_s, D_MODEL)
    cos_p, sin_p = _rope_tables(jnp.arange(seq, dtype=jnp.int32))
    cos_s, sin_s = _rope_tables(jnp.full((n_s,), past_len, jnp.int32))
    vec = lambda t: t.reshape(1, -1)
    g_final = vec(norm_final)

    def rows_major(t, heads):
        return jnp.transpose(t.reshape(batch, heads, HEAD_DIM, seq), (0, 3, 1, 2))

    new_kv = {}
    for i in range(depth):
        j = i // 2
        g_mix, g_ffn, g_ple = vec(norm_mix[i]), vec(norm_ffn[i]), vec(norm_ple[i])
        if i % 2 == 0:
            lam_init = 0.8 - 0.6 * math.exp(-0.3 * i)
            lams = [vec(lambda_q1[j]), vec(lambda_k1[j]), vec(lambda_q2[j]), vec(lambda_k2[j])]
            sub = vec(subln_diff[j])
            w_qkv = w_qkv_diff[j].astype(BF16)
            w_o = w_o_diff[j].astype(BF16)
            qp, kp_t, vp = _qkv(xp, g_mix, w_qkv, cos_p, sin_p, True, seq, k_t=True)
            qs, kn, vn = _qkv(xs, g_mix, w_qkv, cos_s, sin_s, True)
            op = _diff_attn_prompt(qp, kp_t, vp, sub, lams, batch, seq, lam_init)
            kp, vp = rows_major(kp_t, 2 * N_HEADS_A), vp.reshape(batch, seq, N_HEADS_A, DV_A)
            os_ = _diff_decode(qs, kn, vn, _keys_last(cache_k_diff[j]),
                               cache_v_diff[j].reshape(-1, PAGE_SIZE * N_HEADS_A, DV_A),
                               page_table, sub, lams, lam_init)
            heads_k, heads_v = (2 * N_HEADS_A, HEAD_DIM), (N_HEADS_A, DV_A)
            tag = "diff"
        else:
            w_qkv = w_qkv_stick[j].astype(BF16)
            w_o = w_o_stick[j].astype(BF16)
            qp, kp_t, vp_t = _qkv(xp, g_mix, w_qkv, cos_p, sin_p, False, seq, k_t=True, v_t=True)
            qs, kn, vn = _qkv(xs, g_mix, w_qkv, cos_s, sin_s, False)
            op = _stick_attn_prompt(qp, kp_t, vp_t, batch, seq)
            kp, vp = rows_major(kp_t, N_HEADS_B), rows_major(vp_t, N_HEADS_B)
            os_ = _stick_decode(qs, _keys_last(cache_k_stick[j]), _keys_last(cache_v_stick[j]),
                                page_table)
            heads_k = heads_v = (N_HEADS_B, HEAD_DIM)
            tag = "stick"
        new_kv.setdefault(tag, []).append((
            kp, vp,
            kn.reshape(dec_batch, dec_seq, *heads_k), vn.reshape(dec_batch, dec_seq, *heads_v)))
        xp = _proj_residual(op, w_o, xp)
        xs = _proj_residual(os_, w_o, xs)

        wq = peer_w_q[i].astype(BF16)
        sk = peer_sub_keys[i].reshape(2 * PEER_HEADS, PEER_N_KEYS, PEER_HALF).astype(BF16)
        u, v = peer_u[i].astype(BF16), peer_v[i].astype(BF16)
        xp = _peer_mix(xp, g_ffn, u, v, _peer_wsel(*_peer_route(xp, g_ffn, wq, sk)))
        xs = _peer_mix(xs, g_ffn, u, v, _peer_wsel(*_peer_route(xs, g_ffn, wq, sk)))

        wg, wp = ple_w_gate[i].astype(BF16), ple_w_proj[i].astype(BF16)
        last = i == depth - 1
        xp = _ple(xp, p_prompt[i].reshape(n_p, -1), g_ple, wg, wp, g_final, last)
        xs = _ple(xs, p_sample[i].reshape(n_s, -1), g_ple, wg, wp, g_final, last)

    stack = lambda tag, idx: jnp.stack([t[idx] for t in new_kv[tag]])
    return (xp.reshape(batch, seq, D_MODEL), xs.reshape(dec_batch, dec_seq, D_MODEL),
            stack("diff", 0), stack("diff", 1), stack("stick", 0), stack("stick", 1),
            stack("diff", 2), stack("diff", 3), stack("stick", 2), stack("stick", 3))
```

```python
import functools
import math

import jax
import jax.numpy as jnp
import numpy as np
from jax import lax
from jax.experimental import pallas as pl
from jax.experimental.pallas import tpu as pltpu

F32 = jnp.float32
BF16 = jnp.bfloat16

D_MODEL = 1024
HEAD_DIM = 64
N_HEADS_A = 8
N_HEADS_B = 16
DV_A = 2 * HEAD_DIM
ROPE_THETA = 10000.0
NORM_EPS = 1e-6
SUBLN_EPS = 1e-5
PAGE_SIZE = 128
PEER_HEADS = 8
PEER_N_KEYS = 128
PEER_TOPK = 16
PEER_HALF = 128
PEER_HK = PEER_HEADS * PEER_TOPK
SCALE = HEAD_DIM ** -0.5
LOG2_E = math.log2(math.e)

LANES = 128
VMEM_LIMIT = 56 * 1024 * 1024
NEG_INF = float("-inf")


def _cparams(*sem):
    return pltpu.CompilerParams(dimension_semantics=sem, vmem_limit_bytes=VMEM_LIMIT)


def _rms(x, g, eps):
    return x * lax.rsqrt(jnp.mean(x * x, axis=-1, keepdims=True) + eps) * g


def _dot_nt(a, b):
    return lax.dot_general(a, b, (((1,), (1,)), ((), ())), preferred_element_type=F32)


def _row_tile(rows, cap):
    return rows if rows <= cap else cap


def _qkv_body(x_ref, g_ref, w_ref, cos_ref, sin_ref, q_ref, k_ref, v_ref, *, rope, k_t, v_t):
    h = _rms(x_ref[...], g_ref[...], NORM_EPS).astype(BF16)
    y = jnp.dot(h, w_ref[...], preferred_element_type=F32)
    q, k, v = y[:, :D_MODEL], y[:, D_MODEL:2 * D_MODEL], y[:, 2 * D_MODEL:]
    if rope:
        cos = jnp.tile(cos_ref[...], (1, D_MODEL // LANES))
        sin = jnp.tile(sin_ref[...], (1, D_MODEL // LANES))
        lane = lax.broadcasted_iota(jnp.int32, q.shape, 1)
        first_half = (lane % HEAD_DIM) < (HEAD_DIM // 2)

        def rot(t):
            partner = jnp.where(first_half,
                                pltpu.roll(t, D_MODEL - HEAD_DIM // 2, 1),
                                pltpu.roll(t, HEAD_DIM // 2, 1))
            return t * cos + partner * sin

        q, k = rot(q), rot(k)
    q_ref[...] = q
    k_ref[...] = k.T if k_t else k
    v_ref[...] = v.T if v_t else v


def _qkv(x, g, w_bf16, cos, sin, rope, seq=None, k_t=False, v_t=False):
    rows = x.shape[0]
    tm = _row_tile(rows, 256)
    n_tab = cos.shape[0] // tm
    row_spec = pl.BlockSpec((tm, D_MODEL), lambda i: (i, 0))
    tab_spec = pl.BlockSpec((tm, LANES), lambda i: (i % n_tab, 0))
    out = jax.ShapeDtypeStruct((rows, D_MODEL), F32)
    if k_t or v_t:
        per_seq = seq // tm
        t_spec = pl.BlockSpec((None, D_MODEL, tm), lambda i: (i // per_seq, 0, i % per_seq))
        t_out = jax.ShapeDtypeStruct((rows // seq, D_MODEL, seq), F32)
    return pl.pallas_call(
        functools.partial(_qkv_body, rope=rope, k_t=k_t, v_t=v_t),
        grid=(rows // tm,),
        in_specs=[row_spec,
                  pl.BlockSpec((1, D_MODEL), lambda i: (0, 0)),
                  pl.BlockSpec((D_MODEL, 3 * D_MODEL), lambda i: (0, 0)),
                  tab_spec, tab_spec],
        out_specs=[row_spec, t_spec if k_t else row_spec, t_spec if v_t else row_spec],
        out_shape=[out, t_out if k_t else out, t_out if v_t else out],
        compiler_params=_cparams("parallel"),
        name="qkv_proj",
    )(x, g, w_bf16, cos, sin)


def _rope_tables(pos):
    half = HEAD_DIM // 2
    inv = 1.0 / (ROPE_THETA ** (jnp.arange(half, dtype=F32) * (2.0 / HEAD_DIM)))
    ang = pos.astype(F32)[:, None] * inv[None, :]
    cos, sin = jnp.cos(ang), jnp.sin(ang)
    cos_t = jnp.concatenate([cos, cos, cos, cos], axis=1)
    sin_t = jnp.concatenate([-sin, sin, -sin, sin], axis=1)
    return cos_t, sin_t


def _proj_body(a_ref, w_ref, r_ref, o_ref):
    o_ref[...] = r_ref[...] + jnp.dot(a_ref[...].astype(BF16), w_ref[...],
                                      preferred_element_type=F32)


def _proj_residual(a, w_bf16, res):
    rows = a.shape[0]
    tm = _row_tile(rows, 512)
    row_spec = pl.BlockSpec((tm, D_MODEL), lambda i: (i, 0))
    return pl.pallas_call(
        _proj_body,
        grid=(rows // tm,),
        in_specs=[row_spec, pl.BlockSpec((D_MODEL, D_MODEL), lambda i: (0, 0)), row_spec],
        out_specs=row_spec,
        out_shape=jax.ShapeDtypeStruct((rows, D_MODEL), F32),
        compiler_params=_cparams("parallel"),
        name="out_proj",
    )(a, w_bf16, res)


def _ple_body(x_ref, p_ref, g_ref, wg_ref, wp_ref, gf_ref, o_ref, *, final_norm):
    x = x_ref[...]
    h = _rms(x, g_ref[...], NORM_EPS).astype(BF16)
    gate = jax.nn.sigmoid(jnp.dot(h, wg_ref[...], preferred_element_type=F32))
    proj = jnp.dot(p_ref[...].astype(BF16), wp_ref[...], preferred_element_type=F32)
    y = x + gate * proj
    if final_norm:
        y = _rms(y, gf_ref[...], NORM_EPS)
    o_ref[...] = y


def _ple(x, p, g, wg_bf16, wp_bf16, g_final, final_norm):
    rows = x.shape[0]
    ple_dim = p.shape[1]
    tm = _row_tile(rows, 512)
    row_spec = pl.BlockSpec((tm, D_MODEL), lambda i: (i, 0))
    vec_spec = pl.BlockSpec((1, D_MODEL), lambda i: (0, 0))
    return pl.pallas_call(
        functools.partial(_ple_body, final_norm=final_norm),
        grid=(rows // tm,),
        in_specs=[row_spec, pl.BlockSpec((tm, ple_dim), lambda i: (i, 0)), vec_spec,
                  pl.BlockSpec((D_MODEL, D_MODEL), lambda i: (0, 0)),
                  pl.BlockSpec((ple_dim, D_MODEL), lambda i: (0, 0)), vec_spec],
        out_specs=row_spec,
        out_shape=jax.ShapeDtypeStruct((rows, D_MODEL), F32),
        compiler_params=_cparams("parallel"),
        name="ple",
    )(x, p, g, wg_bf16, wp_bf16, g_final)


def _stack_heads(q):
    lane = lax.broadcasted_iota(jnp.int32, q.shape, 1)
    q0 = jnp.where(lane < HEAD_DIM, q, 0.0)
    q1 = jnp.where(lane >= HEAD_DIM, q, 0.0)
    return jnp.concatenate([q0, q1], axis=0)


def _causal_steps(nq, descending):
    qi, kj = [], []
    for i in range(nq):
        for j in (range(i, -1, -1) if descending else range(i + 1)):
            qi.append(i)
            kj.append(j)
    return np.asarray(qi, np.int32), np.asarray(kj, np.int32)


def _diag_mask(tq, strict, stacked=1):
    row = lax.broadcasted_iota(jnp.int32, (stacked * tq, tq), 0) & (tq - 1)
    col = lax.broadcasted_iota(jnp.int32, (stacked * tq, tq), 1)
    return col < row if strict else col <= row


def _across(stat, width):
    return jnp.tile(stat, (1, width // LANES))


def _attn_tile(seq, cap):
    tq = min(seq, cap)
    assert seq % tq == 0 and tq & (tq - 1) == 0 and tq % LANES == 0
    return tq


def _diff_attn_body(qi_ref, kj_ref, q_ref, kt_ref, v_ref, sub_ref, lq1_ref, lk1_ref, lq2_ref,
                    lk2_ref, o_ref, qs_sc, m_sc, l_sc, acc_sc, *, tq, lam_init):
    t = pl.program_id(2)
    i, j = qi_ref[t], kj_ref[t]

    @pl.when(j == 0)
    def _():
        qs_sc[...] = _stack_heads(q_ref[...] * (SCALE * LOG2_E)).astype(BF16)
        m_sc[...] = jnp.full(m_sc.shape, NEG_INF, F32)
        l_sc[...] = jnp.zeros(l_sc.shape, F32)
        acc_sc[...] = jnp.zeros(acc_sc.shape, F32)

    def update(diagonal):
        s = jnp.dot(qs_sc[...], kt_ref[...].astype(BF16), preferred_element_type=F32)
        if diagonal:
            s = jnp.where(_diag_mask(tq, strict=False, stacked=2), s, NEG_INF)
        m_prev = m_sc[...]
        m_new = jnp.maximum(m_prev, jnp.max(s, axis=1, keepdims=True))
        alpha = jnp.exp2(m_prev - m_new)
        p = jnp.exp2(s - _across(m_new, tq))
        l_sc[...] = alpha * l_sc[...] + jnp.sum(p, axis=1, keepdims=True)
        acc_sc[...] = alpha * acc_sc[...] + jnp.dot(
            p.astype(BF16), v_ref[...].astype(BF16), preferred_element_type=F32)
        m_sc[...] = m_new

    @pl.when(j < i)
    def _():
        update(False)

    @pl.when(j == i)
    def _():
        update(True)
        lam = (jnp.exp(jnp.sum(lq1_ref[...] * lk1_ref[...], keepdims=True))
               - jnp.exp(jnp.sum(lq2_ref[...] * lk2_ref[...], keepdims=True)) + lam_init)
        a = acc_sc[...] / l_sc[...]
        o = a[:tq] - lam * a[tq:]
        o_ref[...] = _rms(o, sub_ref[...], SUBLN_EPS) * (1.0 - lam_init)


def _diff_attn_prompt(q, k_t, v, subln, lams, batch, seq, lam_init):
    tq = _attn_tile(seq, 512)
    nq = seq // tq
    qi, kj = _causal_steps(nq, descending=False)
    const = lambda shape: pl.BlockSpec(shape, lambda b, h, t, qi, kj: (0, 0))
    q_spec = pl.BlockSpec((tq, LANES), lambda b, h, t, qi, kj: (b * nq + qi[t], h))
    kt_spec = pl.BlockSpec((None, LANES, tq), lambda b, h, t, qi, kj: (b, h, kj[t]))
    v_spec = pl.BlockSpec((tq, LANES), lambda b, h, t, qi, kj: (b * nq + kj[t], h))
    vec64 = const((1, HEAD_DIM))
    return pl.pallas_call(
        functools.partial(_diff_attn_body, tq=tq, lam_init=lam_init),
        grid_spec=pltpu.PrefetchScalarGridSpec(
            num_scalar_prefetch=2,
            grid=(batch, N_HEADS_A, len(qi)),
            in_specs=[q_spec, kt_spec, v_spec, const((1, DV_A)), vec64, vec64, vec64, vec64],
            out_specs=q_spec,
            scratch_shapes=[pltpu.VMEM((2 * tq, LANES), BF16),
                            pltpu.VMEM((2 * tq, LANES), F32),
                            pltpu.VMEM((2 * tq, LANES), F32),
                            pltpu.VMEM((2 * tq, LANES), F32)]),
        out_shape=jax.ShapeDtypeStruct((batch * seq, D_MODEL), F32),
        compiler_params=_cparams("parallel", "parallel", "arbitrary"),
        name="diff_attn_prompt",
    )(qi, kj, q, k_t, v, subln, *lams)


def _softplus(z):
    return jnp.maximum(z, 0.0) + jnp.log(1.0 + jnp.exp(jnp.minimum(z, -z)))


def _softplus2(z2):
    return jnp.maximum(z2, 0.0) + jnp.log2(1.0 + jnp.exp2(jnp.minimum(z2, -z2)))


def _split_bf16(x):
    hi = x.astype(BF16)
    lo = (x - hi.astype(F32)).astype(BF16)
    return hi, lo


def _neg_tri(n, copies):
    r = lax.broadcasted_iota(jnp.int32, (copies * n, n), 0) & (n - 1)
    c = lax.broadcasted_iota(jnp.int32, (copies * n, n), 1)
    return jnp.where(r > c, -1.0, 0.0).astype(BF16)


def _later_keys_tail(cost, carry, tri):
    n = cost.shape[1]
    if tri.shape[0] == 2 * n:
        operand = jnp.concatenate(_split_bf16(cost), axis=1)
    else:
        operand = cost.astype(BF16)
    return jnp.dot(operand, tri, preferred_element_type=F32) + _across(carry, n)


STICK_SUB = 256
STICK_TRI_COPIES = 1


def _stick_attn_body(qi_ref, kj_ref, q_ref, kt_ref, vt_ref, o_ref, qs_sc, c_sc, acc_sc, tri_sc, *, tq):
    t = pl.program_id(2)
    i, j = qi_ref[t], kj_ref[t]
    sub = tri_sc.shape[1]

    @pl.when(j == i)
    def _():
        qs_sc[...] = _stack_heads(q_ref[...] * (SCALE * LOG2_E)).astype(BF16)
        c_sc[...] = jnp.zeros(c_sc.shape, F32)
        acc_sc[...] = jnp.zeros(acc_sc.shape, F32)
        tri_sc[...] = _neg_tri(sub, tri_sc.shape[0] // sub)

    def update(diagonal):
        z = jnp.dot(qs_sc[...], kt_ref[...].astype(BF16), preferred_element_type=F32)
        sp = _softplus2(z)
        if diagonal:
            mask = _diag_mask(tq, strict=True, stacked=2)
            cost = jnp.where(mask, sp, 0.0)
        else:
            cost = sp
        carry = c_sc[...]
        tails = []
        for blk in reversed(range(tq // sub)):
            cost_b = cost[:, blk * sub:(blk + 1) * sub]
            tails.append(_later_keys_tail(cost_b, carry, tri_sc[...]))
            carry = carry - jnp.sum(cost_b, axis=1, keepdims=True)
        c_sc[...] = carry
        w = jnp.exp2(z - sp + jnp.concatenate(tails[::-1], axis=1))
        if diagonal:
            w = jnp.where(mask, w, 0.0)
        acc_sc[...] = acc_sc[...] + _dot_nt(w.astype(BF16), vt_ref[...].astype(BF16))

    @pl.when(j < i)
    def _():
        update(False)

    @pl.when(j == i)
    def _():
        update(True)

    @pl.when(j == 0)
    def _():
        a = acc_sc[...]
        lane = lax.broadcasted_iota(jnp.int32, (tq, LANES), 1)
        o_ref[...] = jnp.where(lane < HEAD_DIM, a[:tq], a[tq:])


def _stick_attn_prompt(q, k_t, v_t, batch, seq):
    tq = _attn_tile(seq, 512)
    sub = min(tq, STICK_SUB)
    nq = seq // tq
    qi, kj = _causal_steps(nq, descending=True)
    q_spec = pl.BlockSpec((tq, LANES), lambda b, h, t, qi, kj: (b * nq + qi[t], h))
    kv_spec = pl.BlockSpec((None, LANES, tq), lambda b, h, t, qi, kj: (b, h, kj[t]))
    return pl.pallas_call(
        functools.partial(_stick_attn_body, tq=tq),
        grid_spec=pltpu.PrefetchScalarGridSpec(
            num_scalar_prefetch=2,
            grid=(batch, N_HEADS_B // 2, len(qi)),
            in_specs=[q_spec, kv_spec, kv_spec],
            out_specs=q_spec,
            scratch_shapes=[pltpu.VMEM((2 * tq, LANES), BF16),
                            pltpu.VMEM((2 * tq, LANES), F32),
                            pltpu.VMEM((2 * tq, LANES), F32),
                            pltpu.VMEM((STICK_TRI_COPIES * sub, sub), BF16)]),
        out_shape=jax.ShapeDtypeStruct((batch * seq, D_MODEL), F32),
        compiler_params=_cparams("parallel", "parallel", "arbitrary"),
        name="stick_attn_prompt",
    )(qi, kj, q, k_t, v_t)


def _topk_axis0(s, order, payloads, k):
    big = jnp.iinfo(jnp.int32).max if order.dtype == jnp.int32 else jnp.inf
    vals, wins, picked = [], [], [[] for _ in payloads]
    for _ in range(k):
        m = jnp.max(s, axis=0, keepdims=True)
        win = jnp.min(jnp.where(s == m, order, big), axis=0, keepdims=True)
        hit = order == win
        vals.append(m)
        wins.append(win)
        for out, pay in zip(picked, payloads):
            out.append(jnp.sum(jnp.where(hit, pay, 0.0), axis=0, keepdims=True))
        s = jnp.where(hit, NEG_INF, s)
    cat = lambda rows: jnp.concatenate(rows, axis=0)
    return cat(vals), cat(wins), [cat(p) for p in picked]


SUB = 8


def _pair_candidates():
    pairs = [(0, j) for j in range(SUB)] + [(0, j) for j in range(SUB, 2 * SUB)]
    for i in range(1, SUB):
        pairs += [(i, j) for j in range(SUB)]
    pairs += [(i, 0) for i in range(SUB, 2 * SUB)]
    flat = np.asarray([i * PEER_TOPK + j for i, j in pairs], np.float32)
    bias = np.asarray([0.0 if (i + 1) * (j + 1) <= PEER_TOPK else NEG_INF for i, j in pairs],
                      np.float32)
    return flat, bias


def _by_first_rank(x):
    t = x.shape[1]
    rep = lambda i: jnp.broadcast_to(x[i:i + 1], (SUB, t))
    return jnp.concatenate([rep(0)] + [rep(i) for i in range(SUB)] + [x[SUB:]], axis=0)


def _by_second_rank(y):
    t = y.shape[1]
    return jnp.concatenate([y[:SUB], y[SUB:]] + [y[:SUB]] * (SUB - 1)
                           + [jnp.broadcast_to(y[0:1], (SUB, t))], axis=0)


def _peer_route_body(x_ref, g_ref, wq_ref, sk_ref, flat_ref, bias_ref, a_ref, b_ref, gate_ref,
                     q_sc, a_sc, b_sc, gate_sc):
    tm = x_ref.shape[0]
    h = _rms(x_ref[...], g_ref[...], NORM_EPS).astype(BF16)
    q_sc[...] = jnp.dot(h, wq_ref[...], preferred_element_type=F32).astype(BF16)
    key_id = lax.broadcasted_iota(jnp.int32, (PEER_N_KEYS, tm), 0)

    def head(hd, carry):
        tops = []
        for c in range(2):
            col = pl.multiple_of((hd * 2 + c) * PEER_HALF, PEER_HALF)
            s_t = _dot_nt(sk_ref[hd * 2 + c], q_sc[:, pl.ds(col, PEER_HALF)])
            vals, ids, _ = _topk_axis0(s_t, key_id, [], PEER_TOPK)
            tops.append((vals, ids.astype(F32)))
        (s1, i1), (s2, i2) = tops
        cand = _by_first_rank(s1) + _by_second_rank(s2) + bias_ref[...]
        top_s, _, (ea, eb) = _topk_axis0(cand, flat_ref[...],
                                         [_by_first_rank(i1), _by_second_rank(i2)], PEER_TOPK)
        e = jnp.exp(top_s - jnp.max(top_s, axis=0, keepdims=True))
        gate = e / jnp.sum(e, axis=0, keepdims=True)
        row = pl.multiple_of(hd * PEER_TOPK, PEER_TOPK)
        a_sc[pl.ds(row, PEER_TOPK), :] = ea
        b_sc[pl.ds(row, PEER_TOPK), :] = eb
        gate_sc[pl.ds(row, PEER_TOPK), :] = gate
        return carry

    lax.fori_loop(0, PEER_HEADS, head, 0, unroll=4)
    a_ref[...] = a_sc[...].T
    b_ref[...] = b_sc[...].T
    gate_ref[...] = gate_sc[...].T


def _peer_route(x, g, wq_bf16, sk_bf16):
    rows = x.shape[0]
    tm = _row_tile(rows, 256)
    qdim = wq_bf16.shape[1]
    row_spec = pl.BlockSpec((tm, PEER_HK), lambda i: (i, 0))
    out = jax.ShapeDtypeStruct((rows, PEER_HK), F32)
    flat, bias = (np.ascontiguousarray(np.broadcast_to(t[:, None], (t.shape[0], tm)))
                  for t in _pair_candidates())
    tab_spec = pl.BlockSpec(flat.shape, lambda i: (0, 0))
    return pl.pallas_call(
        _peer_route_body,
        grid=(rows // tm,),
        in_specs=[pl.BlockSpec((tm, D_MODEL), lambda i: (i, 0)),
                  pl.BlockSpec((1, D_MODEL), lambda i: (0, 0)),
                  pl.BlockSpec((D_MODEL, qdim), lambda i: (0, 0)),
                  pl.BlockSpec(sk_bf16.shape, lambda i: (0, 0, 0)),
                  tab_spec, tab_spec],
        out_specs=[row_spec, row_spec, row_spec],
        out_shape=[out, out, out],
        scratch_shapes=[pltpu.VMEM((tm, qdim), BF16),
                        pltpu.VMEM((PEER_HK, tm), F32),
                        pltpu.VMEM((PEER_HK, tm), F32),
                        pltpu.VMEM((PEER_HK, tm), F32)],
        compiler_params=_cparams("parallel"),
        name="peer_route",
    )(x, g, wq_bf16, sk_bf16, flat, bias)


def _peer_wsel_body(a_ref, b_ref, gate_ref, w_ref):
    tb = a_ref.shape[0]
    sub = lax.broadcasted_iota(jnp.int32, (PEER_N_KEYS, PEER_HK), 0).astype(F32)

    def token(t, carry):
        a = a_ref[pl.ds(t, 1), :]
        b = b_ref[pl.ds(t, 1), :]
        gt = 0.5 * gate_ref[pl.ds(t, 1), :]
        p_t = jnp.where(sub == a, gt, 0.0).astype(BF16)
        q_t = jnp.where(sub == b, 1.0, 0.0).astype(BF16)
        row = pl.multiple_of(t * PEER_N_KEYS, PEER_N_KEYS)
        w_ref[pl.ds(row, PEER_N_KEYS), :] = _dot_nt(p_t, q_t)
        return carry

    lax.fori_loop(0, tb, token, 0, unroll=True)


def _peer_wsel(a, b, gate):
    rows = a.shape[0]
    tb = _row_tile(rows, 64)
    row_spec = pl.BlockSpec((tb, PEER_HK), lambda i: (i, 0))
    w = pl.pallas_call(
        _peer_wsel_body,
        grid=(rows // tb,),
        in_specs=[row_spec, row_spec, row_spec],
        out_specs=pl.BlockSpec((tb * PEER_N_KEYS, PEER_N_KEYS), lambda i: (i, 0)),
        out_shape=jax.ShapeDtypeStruct((rows * PEER_N_KEYS, PEER_N_KEYS), F32),
        compiler_params=_cparams("parallel"),
        name="peer_wsel",
    )(a, b, gate)
    return w.reshape(rows, PEER_N_KEYS, PEER_N_KEYS)


MIX_FIRST_KEYS = 16


def _peer_mix_body(x_ref, g_ref, u_ref, v_ref, w_ref, o_ref, h_sc, acc_sc):
    j = pl.program_id(1)

    @pl.when(j == 0)
    def _():
        h_sc[...] = _rms(x_ref[...], g_ref[...], NORM_EPS).astype(BF16)
        acc_sc[...] = jnp.zeros(acc_sc.shape, F32)

    s = _dot_nt(h_sc[...], u_ref[...])
    act = s * (1.0 + lax.erf(s * math.sqrt(0.5)))
    z = jnp.concatenate(
        [act[:, r * PEER_N_KEYS:(r + 1) * PEER_N_KEYS] * w_ref[:, r, :]
         for r in range(MIX_FIRST_KEYS)], axis=1).astype(BF16)
    acc_sc[...] = acc_sc[...] + jnp.dot(z, v_ref[...], preferred_element_type=F32)

    @pl.when(j == pl.num_programs(1) - 1)
    def _():
        o_ref[...] = x_ref[...] + acc_sc[...]


def _peer_mix(x, g, u_bf16, v_bf16, wsel):
    rows = x.shape[0]
    n_exp = u_bf16.shape[0]
    tm = _row_tile(rows, 512)
    te = MIX_FIRST_KEYS * PEER_N_KEYS
    row_spec = pl.BlockSpec((tm, D_MODEL), lambda i, j: (i, 0))
    exp_spec = pl.BlockSpec((te, D_MODEL), lambda i, j: (j, 0))
    return pl.pallas_call(
        _peer_mix_body,
        grid=(rows // tm, n_exp // te),
        in_specs=[row_spec, pl.BlockSpec((1, D_MODEL), lambda i, j: (0, 0)),
                  exp_spec, exp_spec,
                  pl.BlockSpec((tm, MIX_FIRST_KEYS, PEER_N_KEYS), lambda i, j: (i, j, 0))],
        out_specs=row_spec,
        out_shape=jax.ShapeDtypeStruct((rows, D_MODEL), F32),
        scratch_shapes=[pltpu.VMEM((tm, D_MODEL), BF16), pltpu.VMEM((tm, D_MODEL), F32)],
        compiler_params=_cparams("parallel", "arbitrary"),
        name="peer_mix",
    )(x, g, u_bf16, v_bf16, wsel)


DEC_REP = 8
DEC_ROWS = N_HEADS_B * DEC_REP
PAGES_PER_STEP = 8


def _head_scores(q_ref, kt_refs):
    return jnp.concatenate(
        [jnp.dot(q_ref[h].astype(BF16),
                 jnp.concatenate([kt[h] for kt in kt_refs], axis=1).astype(BF16),
                 preferred_element_type=F32)
         for h in range(N_HEADS_B)], axis=0)


def _diff_decode_body(pt_ref, q_ref, knew_ref, vnew_ref, sub_ref, lq1_ref, lk1_ref, lq2_ref, lk2_ref,
                      *rest, lam_init):
    kt_refs = rest[:PAGES_PER_STEP]
    v_refs = rest[PAGES_PER_STEP:2 * PAGES_PER_STEP]
    o_ref, m_sc, l_sc, acc_sc = rest[2 * PAGES_PER_STEP:]
    step = pl.program_id(1)

    @pl.when(step == 0)
    def _():
        m_sc[...] = jnp.full(m_sc.shape, NEG_INF, F32)
        l_sc[...] = jnp.zeros(l_sc.shape, F32)
        acc_sc[...] = jnp.zeros(acc_sc.shape, F32)

    s = _head_scores(q_ref, kt_refs) * SCALE
    m_prev = m_sc[...]
    m_new = jnp.maximum(m_prev, jnp.max(s, axis=1, keepdims=True))
    alpha = jnp.exp(m_prev - m_new)
    p = jnp.exp(s - m_new)
    l_sc[...] = alpha * l_sc[...] + jnp.sum(p, axis=1, keepdims=True)
    m_sc[...] = m_new
    p = p.astype(BF16)
    pv = []
    for h in range(N_HEADS_A):
        rows = slice(2 * DEC_REP * h, 2 * DEC_REP * (h + 1))
        v_h = jnp.concatenate(
            [v_ref[pl.ds(h, PAGE_SIZE, stride=N_HEADS_A), :] for v_ref in v_refs], axis=0)
        pv.append(jnp.dot(p[rows], v_h.astype(BF16), preferred_element_type=F32))
    acc_sc[...] = alpha * acc_sc[...] + jnp.concatenate(pv, axis=0)

    @pl.when(step == pl.num_programs(1) - 1)
    def _():
        s_new = jnp.sum(q_ref[...] * knew_ref[...], axis=-1,
                        keepdims=True).reshape(DEC_ROWS, 1) * SCALE
        m_f = jnp.maximum(m_sc[...], s_new)
        a = jnp.exp(m_sc[...] - m_f)
        p_new = jnp.exp(s_new - m_f)
        l_f = a * l_sc[...] + p_new
        o = (a * acc_sc[...] + p_new * vnew_ref[...]) / l_f
        lam = (jnp.exp(jnp.sum(lq1_ref[...] * lk1_ref[...], keepdims=True))
               - jnp.exp(jnp.sum(lq2_ref[...] * lk2_ref[...], keepdims=True)) + lam_init)
        for h in range(N_HEADS_A):
            r0 = 2 * DEC_REP * h
            o_h = o[r0:r0 + DEC_REP] - lam * o[r0 + DEC_REP:r0 + 2 * DEC_REP]
            o_ref[h * DEC_REP:(h + 1) * DEC_REP, :] = (
                _rms(o_h, sub_ref[...], SUBLN_EPS) * (1.0 - lam_init))


def _diff_decode(q, k_new, v_new, cache_kt, cache_v, page_table, subln, lams, lam_init):
    batch, n_pages = page_table.shape
    steps = n_pages // PAGES_PER_STEP
    rep = lambda t: jnp.broadcast_to(t.reshape(batch, N_HEADS_B, 1, HEAD_DIM),
                                     (batch, N_HEADS_B, DEC_REP, HEAD_DIM))
    v_rep = jnp.broadcast_to(v_new.reshape(batch, N_HEADS_A, 1, DV_A),
                             (batch, N_HEADS_A, 2 * DEC_REP, DV_A)).reshape(batch, DEC_ROWS, DV_A)
    q_spec = pl.BlockSpec((None, N_HEADS_B, DEC_REP, HEAD_DIM), lambda b, s, pt: (b, 0, 0, 0))
    vec64 = pl.BlockSpec((1, HEAD_DIM), lambda b, s, pt: (0, 0))

    def page(pg):
        return lambda b, s, pt: (pt[b, s * PAGES_PER_STEP + pg], 0, 0, 0)

    def page3(pg):
        return lambda b, s, pt: (pt[b, s * PAGES_PER_STEP + pg], 0, 0)

    kt_specs = [pl.BlockSpec((None, N_HEADS_B, HEAD_DIM, PAGE_SIZE), page(pg))
                for pg in range(PAGES_PER_STEP)]
    v_specs = [pl.BlockSpec((None, PAGE_SIZE * N_HEADS_A, DV_A), page3(pg))
               for pg in range(PAGES_PER_STEP)]
    out = pl.pallas_call(
        functools.partial(_diff_decode_body, lam_init=lam_init),
        grid_spec=pltpu.PrefetchScalarGridSpec(
            num_scalar_prefetch=1,
            grid=(batch, steps),
            in_specs=[q_spec, q_spec,
                      pl.BlockSpec((None, DEC_ROWS, DV_A), lambda b, s, pt: (b, 0, 0)),
                      pl.BlockSpec((1, DV_A), lambda b, s, pt: (0, 0)),
                      vec64, vec64, vec64, vec64] + kt_specs + v_specs,
            out_specs=pl.BlockSpec((None, N_HEADS_A * DEC_REP, DV_A), lambda b, s, pt: (b, 0, 0)),
            scratch_shapes=[pltpu.VMEM((DEC_ROWS, 1), F32),
                            pltpu.VMEM((DEC_ROWS, 1), F32),
                            pltpu.VMEM((DEC_ROWS, DV_A), F32)]),
        out_shape=jax.ShapeDtypeStruct((batch, N_HEADS_A * DEC_REP, DV_A), F32),
        compiler_params=_cparams("parallel", "arbitrary"),
        name="diff_attn_decode",
    )(page_table, rep(q), rep(k_new), v_rep, subln, *lams,
      *([cache_kt] * PAGES_PER_STEP), *([cache_v] * PAGES_PER_STEP))
    return out[:, ::DEC_REP, :].reshape(batch, D_MODEL)


def _stick_decode_body(pt_ref, q_ref, *rest):
    kt_refs = rest[:PAGES_PER_STEP]
    vt_refs = rest[PAGES_PER_STEP:2 * PAGES_PER_STEP]
    o_ref, c_sc, acc_sc = rest[2 * PAGES_PER_STEP:]
    step = pl.program_id(1)

    @pl.when(step == 0)
    def _():
        c_sc[...] = jnp.zeros(c_sc.shape, F32)
        acc_sc[...] = jnp.zeros(acc_sc.shape, F32)

    tri2 = _neg_tri(PAGE_SIZE, 2)
    carry = c_sc[...]
    z = _head_scores(q_ref, kt_refs) * SCALE
    sp = _softplus(z)
    tails = []
    for pg in range(PAGES_PER_STEP):
        sp_pg = sp[:, pg * PAGE_SIZE:(pg + 1) * PAGE_SIZE]
        tails.append(_later_keys_tail(sp_pg, carry, tri2))
        carry = carry - jnp.sum(sp_pg, axis=1, keepdims=True)
    c_sc[...] = carry
    w = jnp.exp(z - sp + jnp.concatenate(tails, axis=1)).astype(BF16)
    acc_sc[...] = acc_sc[...] + jnp.concatenate(
        [_dot_nt(w[h * DEC_REP:(h + 1) * DEC_REP],
                 jnp.concatenate([vt[h] for vt in vt_refs], axis=1).astype(BF16))
         for h in range(N_HEADS_B)], axis=0)

    @pl.when(step == pl.num_programs(1) - 1)
    def _():
        o_ref[...] = acc_sc[...]


def _stick_decode(q, cache_kt, cache_vt, page_table):
    batch, n_pages = page_table.shape
    steps = n_pages // PAGES_PER_STEP
    q_rep = jnp.broadcast_to(q.reshape(batch, N_HEADS_B, 1, HEAD_DIM),
                             (batch, N_HEADS_B, DEC_REP, HEAD_DIM))

    def page(pg):
        return lambda b, s, pt: (pt[b, n_pages - 1 - (s * PAGES_PER_STEP + pg)], 0, 0, 0)

    specs = [pl.BlockSpec((None, N_HEADS_B, HEAD_DIM, PAGE_SIZE), page(pg))
             for pg in range(PAGES_PER_STEP)]
    out = pl.pallas_call(
        _stick_decode_body,
        grid_spec=pltpu.PrefetchScalarGridSpec(
            num_scalar_prefetch=1,
            grid=(batch, steps),
            in_specs=[pl.BlockSpec((None, N_HEADS_B, DEC_REP, HEAD_DIM),
                                   lambda b, s, pt: (b, 0, 0, 0))] + specs + specs,
            out_specs=pl.BlockSpec((None, DEC_ROWS, HEAD_DIM), lambda b, s, pt: (b, 0, 0)),
            scratch_shapes=[pltpu.VMEM((DEC_ROWS, LANES), F32),
                            pltpu.VMEM((DEC_ROWS, HEAD_DIM), F32)]),
        out_shape=jax.ShapeDtypeStruct((batch, DEC_ROWS, HEAD_DIM), F32),
        compiler_params=_cparams("parallel", "arbitrary"),
        name="stick_attn_decode",
    )(page_table, q_rep, *([cache_kt] * PAGES_PER_STEP), *([cache_vt] * PAGES_PER_STEP))
    return out[:, ::DEC_REP, :].reshape(batch, D_MODEL)


def _keys_last(cache):
    return jnp.transpose(cache, (0, 2, 3, 1))


def kernel(x_prompt, x_sample, p_prompt, p_sample, cache_k_diff, cache_v_diff, cache_k_stick, cache_v_stick, page_table, norm_mix, norm_ffn, norm_ple, norm_final, w_qkv_diff, w_o_diff, lambda_q1, lambda_k1, lambda_q2, lambda_k2, subln_diff, w_qkv_stick, w_o_stick, peer_w_q, peer_sub_keys, peer_u, peer_v, ple_w_gate, ple_w_proj):
    batch, seq, _ = x_prompt.shape
    dec_batch, dec_seq, _ = x_sample.shape
    assert dec_seq == 1
    depth = norm_mix.shape[0]
    n_pages = page_table.shape[1]
    past_len = n_pages * PAGE_SIZE
    n_p, n_s = batch * seq, dec_batch * dec_seq

    xp = x_prompt.reshape(n_p, D_MODEL)
    xs = x_sample.reshape(n_s, D_MODEL)
    cos_p, sin_p = _rope_tables(jnp.arange(seq, dtype=jnp.int32))
    cos_s, sin_s = _rope_tables(jnp.full((n_s,), past_len, jnp.int32))
    vec = lambda t: t.reshape(1, -1)
    g_final = vec(norm_final)

    def rows_major(t, heads):
        return jnp.transpose(t.reshape(batch, heads, HEAD_DIM, seq), (0, 3, 1, 2))

    new_kv = {}
    for i in range(depth):
        j = i // 2
        g_mix, g_ffn, g_ple = vec(norm_mix[i]), vec(norm_ffn[i]), vec(norm_ple[i])
        if i % 2 == 0:
            lam_init = 0.8 - 0.6 * math.exp(-0.3 * i)
            lams = [vec(lambda_q1[j]), vec(lambda_k1[j]), vec(lambda_q2[j]), vec(lambda_k2[j])]
            sub = vec(subln_diff[j])
            w_qkv = w_qkv_diff[j].astype(BF16)
            w_o = w_o_diff[j].astype(BF16)
            qp, kp_t, vp = _qkv(xp, g_mix, w_qkv, cos_p, sin_p, True, seq, k_t=True)
            qs, kn, vn = _qkv(xs, g_mix, w_qkv, cos_s, sin_s, True)
            op = _diff_attn_prompt(qp, kp_t, vp, sub, lams, batch, seq, lam_init)
            kp, vp = rows_major(kp_t, 2 * N_HEADS_A), vp.reshape(batch, seq, N_HEADS_A, DV_A)
            os_ = _diff_decode(qs, kn, vn, _keys_last(cache_k_diff[j]),
                               cache_v_diff[j].reshape(-1, PAGE_SIZE * N_HEADS_A, DV_A),
                               page_table, sub, lams, lam_init)
            heads_k, heads_v = (2 * N_HEADS_A, HEAD_DIM), (N_HEADS_A, DV_A)
            tag = "diff"
        else:
            w_qkv = w_qkv_stick[j].astype(BF16)
            w_o = w_o_stick[j].astype(BF16)
            qp, kp_t, vp_t = _qkv(xp, g_mix, w_qkv, cos_p, sin_p, False, seq, k_t=True, v_t=True)
            qs, kn, vn = _qkv(xs, g_mix, w_qkv, cos_s, sin_s, False)
            op = _stick_attn_prompt(qp, kp_t, vp_t, batch, seq)
            kp, vp = rows_major(kp_t, N_HEADS_B), rows_major(vp_t, N_HEADS_B)
            os_ = _stick_decode(qs, _keys_last(cache_k_stick[j]), _keys_last(cache_v_stick[j]),
                                page_table)
            heads_k = heads_v = (N_HEADS_B, HEAD_DIM)
            tag = "stick"
        new_kv.setdefault(tag, []).append((
            kp, vp,
            kn.reshape(dec_batch, dec_seq, *heads_k), vn.reshape(dec_batch, dec_seq, *heads_v)))
        xp = _proj_residual(op, w_o, xp)
        xs = _proj_residual(os_, w_o, xs)

        wq = peer_w_q[i].astype(BF16)
        sk = peer_sub_keys[i].reshape(2 * PEER_HEADS, PEER_N_KEYS, PEER_HALF).astype(BF16)
        u, v = peer_u[i].astype(BF16), peer_v[i].astype(BF16)
        xp = _peer_mix(xp, g_ffn, u, v, _peer_wsel(*_peer_route(xp, g_ffn, wq, sk)))
        xs = _peer_mix(xs, g_ffn, u, v, _peer_wsel(*_peer_route(xs, g_ffn, wq, sk)))

        wg, wp = ple_w_gate[i].astype(BF16), ple_w_proj[i].astype(BF16)
        last = i == depth - 1
        xp = _ple(xp, p_prompt[i].reshape(n_p, -1), g_ple, wg, wp, g_final, last)
        xs = _ple(xs, p_sample[i].reshape(n_s, -1), g_ple, wg, wp, g_final, last)

    stack = lambda tag, idx: jnp.stack([t[idx] for t in new_kv[tag]])
    return (xp.reshape(batch, seq, D_MODEL), xs.reshape(dec_batch, dec_seq, D_MODEL),
            stack("diff", 0), stack("diff", 1), stack("stick", 0), stack("stick", 1),
            stack("diff", 2), stack("diff", 3), stack("stick", 2), stack("stick", 3))
```

```python
import functools
import math

import jax
import jax.numpy as jnp
import numpy as np
from jax import lax
from jax.experimental import pallas as pl
from jax.experimental.pallas import tpu as pltpu

F32 = jnp.float32
BF16 = jnp.bfloat16

D_MODEL = 1024
HEAD_DIM = 64
N_HEADS_A = 8
N_HEADS_B = 16
DV_A = 2 * HEAD_DIM
ROPE_THETA = 10000.0
NORM_EPS = 1e-6
SUBLN_EPS = 1e-5
PAGE_SIZE = 128
PEER_HEADS = 8
PEER_N_KEYS = 128
PEER_TOPK = 16
PEER_HALF = 128
PEER_HK = PEER_HEADS * PEER_TOPK
SCALE = HEAD_DIM ** -0.5
LOG2_E = math.log2(math.e)

LANES = 128
VMEM_LIMIT = 56 * 1024 * 1024
NEG_INF = float("-inf")


def _cparams(*sem):
    return pltpu.CompilerParams(dimension_semantics=sem, vmem_limit_bytes=VMEM_LIMIT)


def _rms(x, g, eps):
    return x * lax.rsqrt(jnp.mean(x * x, axis=-1, keepdims=True) + eps) * g


def _dot_nt(a, b):
    return lax.dot_general(a, b, (((1,), (1,)), ((), ())), preferred_element_type=F32)


def _row_tile(rows, cap):
    return rows if rows <= cap else cap


def _qkv_body(x_ref, g_ref, w_ref, cos_ref, sin_ref, q_ref, k_ref, v_ref, *, rope, k_t, v_t):
    h = _rms(x_ref[...], g_ref[...], NORM_EPS).astype(BF16)
    y = jnp.dot(h, w_ref[...], preferred_element_type=F32)
    q, k, v = y[:, :D_MODEL], y[:, D_MODEL:2 * D_MODEL], y[:, 2 * D_MODEL:]
    if rope:
        cos = jnp.tile(cos_ref[...], (1, D_MODEL // LANES))
        sin = jnp.tile(sin_ref[...], (1, D_MODEL // LANES))
        lane = lax.broadcasted_iota(jnp.int32, q.shape, 1)
        first_half = (lane % HEAD_DIM) < (HEAD_DIM // 2)

        def rot(t):
            partner = jnp.where(first_half,
                                pltpu.roll(t, D_MODEL - HEAD_DIM // 2, 1),
                                pltpu.roll(t, HEAD_DIM // 2, 1))
            return t * cos + partner * sin

        q, k = rot(q), rot(k)
    q_ref[...] = q
    k_ref[...] = k.T if k_t else k
    v_ref[...] = v.T if v_t else v


def _qkv(x, g, w_bf16, cos, sin, rope, seq=None, k_t=False, v_t=False):
    rows = x.shape[0]
    tm = _row_tile(rows, 256)
    n_tab = cos.shape[0] // tm
    row_spec = pl.BlockSpec((tm, D_MODEL), lambda i: (i, 0))
    tab_spec = pl.BlockSpec((tm, LANES), lambda i: (i % n_tab, 0))
    out = jax.ShapeDtypeStruct((rows, D_MODEL), F32)
    if k_t or v_t:
        per_seq = seq // tm
        t_spec = pl.BlockSpec((None, D_MODEL, tm), lambda i: (i // per_seq, 0, i % per_seq))
        t_out = jax.ShapeDtypeStruct((rows // seq, D_MODEL, seq), F32)
    return pl.pallas_call(
        functools.partial(_qkv_body, rope=rope, k_t=k_t, v_t=v_t),
        grid=(rows // tm,),
        in_specs=[row_spec,
                  pl.BlockSpec((1, D_MODEL), lambda i: (0, 0)),
                  pl.BlockSpec((D_MODEL, 3 * D_MODEL), lambda i: (0, 0)),
                  tab_spec, tab_spec],
        out_specs=[row_spec, t_spec if k_t else row_spec, t_spec if v_t else row_spec],
        out_shape=[out, t_out if k_t else out, t_out if v_t else out],
        compiler_params=_cparams("parallel"),
        name="qkv_proj",
    )(x, g, w_bf16, cos, sin)


def _rope_tables(pos):
    half = HEAD_DIM // 2
    inv = 1.0 / (ROPE_THETA ** (jnp.arange(half, dtype=F32) * (2.0 / HEAD_DIM)))
    ang = pos.astype(F32)[:, None] * inv[None, :]
    cos, sin = jnp.cos(ang), jnp.sin(ang)
    cos_t = jnp.concatenate([cos, cos, cos, cos], axis=1)
    sin_t = jnp.concatenate([-sin, sin, -sin, sin], axis=1)
    return cos_t, sin_t


def _proj_body(a_ref, w_ref, r_ref, o_ref):
    o_ref[...] = r_ref[...] + jnp.dot(a_ref[...].astype(BF16), w_ref[...],
                                      preferred_element_type=F32)


def _proj_residual(a, w_bf16, res):
    rows = a.shape[0]
    tm = _row_tile(rows, 512)
    row_spec = pl.BlockSpec((tm, D_MODEL), lambda i: (i, 0))
    return pl.pallas_call(
        _proj_body,
        grid=(rows // tm,),
        in_specs=[row_spec, pl.BlockSpec((D_MODEL, D_MODEL), lambda i: (0, 0)), row_spec],
        out_specs=row_spec,
        out_shape=jax.ShapeDtypeStruct((rows, D_MODEL), F32),
        compiler_params=_cparams("parallel"),
        name="out_proj",
    )(a, w_bf16, res)


def _ple_body(x_ref, p_ref, g_ref, wg_ref, wp_ref, gf_ref, o_ref, *, final_norm):
    x = x_ref[...]
    h = _rms(x, g_ref[...], NORM_EPS).astype(BF16)
    gate = jax.nn.sigmoid(jnp.dot(h, wg_ref[...], preferred_element_type=F32))
    proj = jnp.dot(p_ref[...].astype(BF16), wp_ref[...], preferred_element_type=F32)
    y = x + gate * proj
    if final_norm:
        y = _rms(y, gf_ref[...], NORM_EPS)
    o_ref[...] = y


def _ple(x, p, g, wg_bf16, wp_bf16, g_final, final_norm):
    rows = x.shape[0]
    ple_dim = p.shape[1]
    tm = _row_tile(rows, 512)
    row_spec = pl.BlockSpec((tm, D_MODEL), lambda i: (i, 0))
    vec_spec = pl.BlockSpec((1, D_MODEL), lambda i: (0, 0))
    return pl.pallas_call(
        functools.partial(_ple_body, final_norm=final_norm),
        grid=(rows // tm,),
        in_specs=[row_spec, pl.BlockSpec((tm, ple_dim), lambda i: (i, 0)), vec_spec,
                  pl.BlockSpec((D_MODEL, D_MODEL), lambda i: (0, 0)),
                  pl.BlockSpec((ple_dim, D_MODEL), lambda i: (0, 0)), vec_spec],
        out_specs=row_spec,
        out_shape=jax.ShapeDtypeStruct((rows, D_MODEL), F32),
        compiler_params=_cparams("parallel"),
        name="ple",
    )(x, p, g, wg_bf16, wp_bf16, g_final)


def _stack_heads(q):
    lane = lax.broadcasted_iota(jnp.int32, q.shape, 1)
    q0 = jnp.where(lane < HEAD_DIM, q, 0.0)
    q1 = jnp.where(lane >= HEAD_DIM, q, 0.0)
    return jnp.concatenate([q0, q1], axis=0)


def _causal_steps(nq, descending):
    qi, kj = [], []
    for i in range(nq):
        for j in (range(i, -1, -1) if descending else range(i + 1)):
            qi.append(i)
            kj.append(j)
    return np.asarray(qi, np.int32), np.asarray(kj, np.int32)


def _diag_mask(tq, strict, stacked=1):
    row = lax.broadcasted_iota(jnp.int32, (stacked * tq, tq), 0) & (tq - 1)
    col = lax.broadcasted_iota(jnp.int32, (stacked * tq, tq), 1)
    return col < row if strict else col <= row


def _across(stat, width):
    return jnp.tile(stat, (1, width // LANES))


def _attn_tile(seq, cap):
    tq = min(seq, cap)
    assert seq % tq == 0 and tq & (tq - 1) == 0 and tq % LANES == 0
    return tq


def _diff_attn_body(qi_ref, kj_ref, q_ref, kt_ref, v_ref, sub_ref, lq1_ref, lk1_ref, lq2_ref,
                    lk2_ref, o_ref, qs_sc, m_sc, l_sc, acc_sc, *, tq, lam_init):
    t = pl.program_id(2)
    i, j = qi_ref[t], kj_ref[t]

    @pl.when(j == 0)
    def _():
        qs_sc[...] = _stack_heads(q_ref[...] * (SCALE * LOG2_E)).astype(BF16)
        m_sc[...] = jnp.full(m_sc.shape, NEG_INF, F32)
        l_sc[...] = jnp.zeros(l_sc.shape, F32)
        acc_sc[...] = jnp.zeros(acc_sc.shape, F32)

    def update(diagonal):
        s = jnp.dot(qs_sc[...], kt_ref[...].astype(BF16), preferred_element_type=F32)
        if diagonal:
            s = jnp.where(_diag_mask(tq, strict=False, stacked=2), s, NEG_INF)
        m_prev = m_sc[...]
        m_new = jnp.maximum(m_prev, jnp.max(s, axis=1, keepdims=True))
        alpha = jnp.exp2(m_prev - m_new)
        p = jnp.exp2(s - _across(m_new, tq))
        l_sc[...] = alpha * l_sc[...] + jnp.sum(p, axis=1, keepdims=True)
        acc_sc[...] = alpha * acc_sc[...] + jnp.dot(
            p.astype(BF16), v_ref[...].astype(BF16), preferred_element_type=F32)
        m_sc[...] = m_new

    @pl.when(j < i)
    def _():
        update(False)

    @pl.when(j == i)
    def _():
        update(True)
        lam = (jnp.exp(jnp.sum(lq1_ref[...] * lk1_ref[...], keepdims=True))
               - jnp.exp(jnp.sum(lq2_ref[...] * lk2_ref[...], keepdims=True)) + lam_init)
        a = acc_sc[...] / l_sc[...]
        o = a[:tq] - lam * a[tq:]
        o_ref[...] = _rms(o, sub_ref[...], SUBLN_EPS) * (1.0 - lam_init)


def _diff_attn_prompt(q, k_t, v, subln, lams, batch, seq, lam_init):
    tq = _attn_tile(seq, 1024)
    nq = seq // tq
    qi, kj = _causal_steps(nq, descending=False)
    const = lambda shape: pl.BlockSpec(shape, lambda b, h, t, qi, kj: (0, 0))
    q_spec = pl.BlockSpec((tq, LANES), lambda b, h, t, qi, kj: (b * nq + qi[t], h))
    kt_spec = pl.BlockSpec((None, LANES, tq), lambda b, h, t, qi, kj: (b, h, kj[t]))
    v_spec = pl.BlockSpec((tq, LANES), lambda b, h, t, qi, kj: (b * nq + kj[t], h))
    vec64 = const((1, HEAD_DIM))
    return pl.pallas_call(
        functools.partial(_diff_attn_body, tq=tq, lam_init=lam_init),
        grid_spec=pltpu.PrefetchScalarGridSpec(
            num_scalar_prefetch=2,
            grid=(batch, N_HEADS_A, len(qi)),
            in_specs=[q_spec, kt_spec, v_spec, const((1, DV_A)), vec64, vec64, vec64, vec64],
            out_specs=q_spec,
            scratch_shapes=[pltpu.VMEM((2 * tq, LANES), BF16),
                            pltpu.VMEM((2 * tq, LANES), F32),
                            pltpu.VMEM((2 * tq, LANES), F32),
                            pltpu.VMEM((2 * tq, LANES), F32)]),
        out_shape=jax.ShapeDtypeStruct((batch * seq, D_MODEL), F32),
        compiler_params=_cparams("parallel", "parallel", "arbitrary"),
        name="diff_attn_prompt",
    )(qi, kj, q, k_t, v, subln, *lams)


def _softplus(z):
    return jnp.maximum(z, 0.0) + jnp.log(1.0 + jnp.exp(jnp.minimum(z, -z)))


def _softplus2(z2):
    return jnp.maximum(z2, 0.0) + jnp.log2(1.0 + jnp.exp2(jnp.minimum(z2, -z2)))


def _split_bf16(x):
    hi = x.astype(BF16)
    lo = (x - hi.astype(F32)).astype(BF16)
    return hi, lo


def _neg_tri(n, copies):
    r = lax.broadcasted_iota(jnp.int32, (copies * n, n), 0) & (n - 1)
    c = lax.broadcasted_iota(jnp.int32, (copies * n, n), 1)
    return jnp.where(r > c, -1.0, 0.0).astype(BF16)


def _later_keys_tail(cost, carry, tri):
    n = cost.shape[1]
    if tri.shape[0] == 2 * n:
        operand = jnp.concatenate(_split_bf16(cost), axis=1)
    else:
        operand = cost.astype(BF16)
    return jnp.dot(operand, tri, preferred_element_type=F32) + _across(carry, n)


STICK_SUB = 256
STICK_TRI_COPIES = 1


def _stick_attn_body(qi_ref, kj_ref, q_ref, kt_ref, vt_ref, o_ref, qs_sc, c_sc, acc_sc, tri_sc, *, tq):
    t = pl.program_id(2)
    i, j = qi_ref[t], kj_ref[t]
    sub = tri_sc.shape[1]

    @pl.when(j == i)
    def _():
        qs_sc[...] = _stack_heads(q_ref[...] * (SCALE * LOG2_E)).astype(BF16)
        c_sc[...] = jnp.zeros(c_sc.shape, F32)
        acc_sc[...] = jnp.zeros(acc_sc.shape, F32)
        tri_sc[...] = _neg_tri(sub, tri_sc.shape[0] // sub)

    def update(diagonal):
        z = jnp.dot(qs_sc[...], kt_ref[...].astype(BF16), preferred_element_type=F32)
        sp = _softplus2(z)
        if diagonal:
            mask = _diag_mask(tq, strict=True, stacked=2)
            cost = jnp.where(mask, sp, 0.0)
        else:
            cost = sp
        carry = c_sc[...]
        tails = []
        for blk in reversed(range(tq // sub)):
            cost_b = cost[:, blk * sub:(blk + 1) * sub]
            tails.append(_later_keys_tail(cost_b, carry, tri_sc[...]))
            carry = carry - jnp.sum(cost_b, axis=1, keepdims=True)
        c_sc[...] = carry
        w = jnp.exp2(z - sp + jnp.concatenate(tails[::-1], axis=1))
        if diagonal:
            w = jnp.where(mask, w, 0.0)
        acc_sc[...] = acc_sc[...] + _dot_nt(w.astype(BF16), vt_ref[...].astype(BF16))

    @pl.when(j < i)
    def _():
        update(False)

    @pl.when(j == i)
    def _():
        update(True)

    @pl.when(j == 0)
    def _():
        a = acc_sc[...]
        lane = lax.broadcasted_iota(jnp.int32, (tq, LANES), 1)
        o_ref[...] = jnp.where(lane < HEAD_DIM, a[:tq], a[tq:])


def _stick_attn_prompt(q, k_t, v_t, batch, seq):
    tq = _attn_tile(seq, 1024)
    sub = min(tq, STICK_SUB)
    nq = seq // tq
    qi, kj = _causal_steps(nq, descending=True)
    q_spec = pl.BlockSpec((tq, LANES), lambda b, h, t, qi, kj: (b * nq + qi[t], h))
    kv_spec = pl.BlockSpec((None, LANES, tq), lambda b, h, t, qi, kj: (b, h, kj[t]))
    return pl.pallas_call(
        functools.partial(_stick_attn_body, tq=tq),
        grid_spec=pltpu.PrefetchScalarGridSpec(
            num_scalar_prefetch=2,
            grid=(batch, N_HEADS_B // 2, len(qi)),
            in_specs=[q_spec, kv_spec, kv_spec],
            out_specs=q_spec,
            scratch_shapes=[pltpu.VMEM((2 * tq, LANES), BF16),
                            pltpu.VMEM((2 * tq, LANES), F32),
                            pltpu.VMEM((2 * tq, LANES), F32),
                            pltpu.VMEM((STICK_TRI_COPIES * sub, sub), BF16)]),
        out_shape=jax.ShapeDtypeStruct((batch * seq, D_MODEL), F32),
        compiler_params=_cparams("parallel", "parallel", "arbitrary"),
        name="stick_attn_prompt",
    )(qi, kj, q, k_t, v_t)


def _topk_axis0(s, order, payloads, k):
    big = jnp.iinfo(jnp.int32).max if order.dtype == jnp.int32 else jnp.inf
    vals, wins, picked = [], [], [[] for _ in payloads]
    for _ in range(k):
        m = jnp.max(s, axis=0, keepdims=True)
        win = jnp.min(jnp.where(s == m, order, big), axis=0, keepdims=True)
        hit = order == win
        vals.append(m)
        wins.append(win)
        for out, pay in zip(picked, payloads):
            out.append(jnp.sum(jnp.where(hit, pay, 0.0), axis=0, keepdims=True))
        s = jnp.where(hit, NEG_INF, s)
    cat = lambda rows: jnp.concatenate(rows, axis=0)
    return cat(vals), cat(wins), [cat(p) for p in picked]


SUB = 8


def _pair_candidates():
    pairs = [(0, j) for j in range(SUB)] + [(0, j) for j in range(SUB, 2 * SUB)]
    for i in range(1, SUB):
        pairs += [(i, j) for j in range(SUB)]
    pairs += [(i, 0) for i in range(SUB, 2 * SUB)]
    flat = np.asarray([i * PEER_TOPK + j for i, j in pairs], np.float32)
    bias = np.asarray([0.0 if (i + 1) * (j + 1) <= PEER_TOPK else NEG_INF for i, j in pairs],
                      np.float32)
    return flat, bias


def _by_first_rank(x):
    t = x.shape[1]
    rep = lambda i: jnp.broadcast_to(x[i:i + 1], (SUB, t))
    return jnp.concatenate([rep(0)] + [rep(i) for i in range(SUB)] + [x[SUB:]], axis=0)


def _by_second_rank(y):
    t = y.shape[1]
    return jnp.concatenate([y[:SUB], y[SUB:]] + [y[:SUB]] * (SUB - 1)
                           + [jnp.broadcast_to(y[0:1], (SUB, t))], axis=0)


def _peer_route_body(x_ref, g_ref, wq_ref, sk_ref, flat_ref, bias_ref, a_ref, b_ref, gate_ref,
                     q_sc, a_sc, b_sc, gate_sc):
    tm = x_ref.shape[0]
    h = _rms(x_ref[...], g_ref[...], NORM_EPS).astype(BF16)
    q_sc[...] = jnp.dot(h, wq_ref[...], preferred_element_type=F32).astype(BF16)
    key_id = lax.broadcasted_iota(jnp.int32, (PEER_N_KEYS, tm), 0)

    def head(hd, carry):
        tops = []
        for c in range(2):
            col = pl.multiple_of((hd * 2 + c) * PEER_HALF, PEER_HALF)
            s_t = _dot_nt(sk_ref[hd * 2 + c], q_sc[:, pl.ds(col, PEER_HALF)])
            vals, ids, _ = _topk_axis0(s_t, key_id, [], PEER_TOPK)
            tops.append((vals, ids.astype(F32)))
        (s1, i1), (s2, i2) = tops
        cand = _by_first_rank(s1) + _by_second_rank(s2) + bias_ref[...]
        top_s, _, (ea, eb) = _topk_axis0(cand, flat_ref[...],
                                         [_by_first_rank(i1), _by_second_rank(i2)], PEER_TOPK)
        e = jnp.exp(top_s - jnp.max(top_s, axis=0, keepdims=True))
        gate = e / jnp.sum(e, axis=0, keepdims=True)
        row = pl.multiple_of(hd * PEER_TOPK, PEER_TOPK)
        a_sc[pl.ds(row, PEER_TOPK), :] = ea
        b_sc[pl.ds(row, PEER_TOPK), :] = eb
        gate_sc[pl.ds(row, PEER_TOPK), :] = gate
        return carry

    lax.fori_loop(0, PEER_HEADS, head, 0, unroll=4)
    a_ref[...] = a_sc[...].T
    b_ref[...] = b_sc[...].T
    gate_ref[...] = gate_sc[...].T


def _peer_route(x, g, wq_bf16, sk_bf16):
    rows = x.shape[0]
    tm = _row_tile(rows, 256)
    qdim = wq_bf16.shape[1]
    row_spec = pl.BlockSpec((tm, PEER_HK), lambda i: (i, 0))
    out = jax.ShapeDtypeStruct((rows, PEER_HK), F32)
    flat, bias = (np.ascontiguousarray(np.broadcast_to(t[:, None], (t.shape[0], tm)))
                  for t in _pair_candidates())
    tab_spec = pl.BlockSpec(flat.shape, lambda i: (0, 0))
    return pl.pallas_call(
        _peer_route_body,
        grid=(rows // tm,),
        in_specs=[pl.BlockSpec((tm, D_MODEL), lambda i: (i, 0)),
                  pl.BlockSpec((1, D_MODEL), lambda i: (0, 0)),
                  pl.BlockSpec((D_MODEL, qdim), lambda i: (0, 0)),
                  pl.BlockSpec(sk_bf16.shape, lambda i: (0, 0, 0)),
                  tab_spec, tab_spec],
        out_specs=[row_spec, row_spec, row_spec],
        out_shape=[out, out, out],
        scratch_shapes=[pltpu.VMEM((tm, qdim), BF16),
                        pltpu.VMEM((PEER_HK, tm), F32),
                        pltpu.VMEM((PEER_HK, tm), F32),
                        pltpu.VMEM((PEER_HK, tm), F32)],
        compiler_params=_cparams("parallel"),
        name="peer_route",
    )(x, g, wq_bf16, sk_bf16, flat, bias)


def _peer_wsel_body(a_ref, b_ref, gate_ref, w_ref):
    tb = a_ref.shape[0]
    sub = lax.broadcasted_iota(jnp.int32, (PEER_N_KEYS, PEER_HK), 0).astype(F32)

    def token(t, carry):
        a = a_ref[pl.ds(t, 1), :]
        b = b_ref[pl.ds(t, 1), :]
        gt = 0.5 * gate_ref[pl.ds(t, 1), :]
        p_t = jnp.where(sub == a, gt, 0.0).astype(BF16)
        q_t = jnp.where(sub == b, 1.0, 0.0).astype(BF16)
        row = pl.multiple_of(t * PEER_N_KEYS, PEER_N_KEYS)
        w_ref[pl.ds(row, PEER_N_KEYS), :] = _dot_nt(p_t, q_t)
        return carry

    lax.fori_loop(0, tb, token, 0, unroll=True)


def _peer_wsel(a, b, gate):
    rows = a.shape[0]
    tb = _row_tile(rows, 64)
    row_spec = pl.BlockSpec((tb, PEER_HK), lambda i: (i, 0))
    w = pl.pallas_call(
        _peer_wsel_body,
        grid=(rows // tb,),
        in_specs=[row_spec, row_spec, row_spec],
        out_specs=pl.BlockSpec((tb * PEER_N_KEYS, PEER_N_KEYS), lambda i: (i, 0)),
        out_shape=jax.ShapeDtypeStruct((rows * PEER_N_KEYS, PEER_N_KEYS), F32),
        compiler_params=_cparams("parallel"),
        name="peer_wsel",
    )(a, b, gate)
    return w.reshape(rows, PEER_N_KEYS, PEER_N_KEYS)


MIX_FIRST_KEYS = 16


def _peer_mix_body(x_ref, g_ref, u_ref, v_ref, w_ref, o_ref, h_sc, acc_sc):
    j = pl.program_id(1)

    @pl.when(j == 0)
    def _():
        h_sc[...] = _rms(x_ref[...], g_ref[...], NORM_EPS).astype(BF16)
        acc_sc[...] = jnp.zeros(acc_sc.shape, F32)

    s = _dot_nt(h_sc[...], u_ref[...])
    act = s * (1.0 + lax.erf(s * math.sqrt(0.5)))
    z = jnp.concatenate(
        [act[:, r * PEER_N_KEYS:(r + 1) * PEER_N_KEYS] * w_ref[:, r, :]
         for r in range(MIX_FIRST_KEYS)], axis=1).astype(BF16)
    acc_sc[...] = acc_sc[...] + jnp.dot(z, v_ref[...], preferred_element_type=F32)

    @pl.when(j == pl.num_programs(1) - 1)
    def _():
        o_ref[...] = x_ref[...] + acc_sc[...]


def _peer_mix(x, g, u_bf16, v_bf16, wsel):
    rows = x.shape[0]
    n_exp = u_bf16.shape[0]
    tm = _row_tile(rows, 512)
    te = MIX_FIRST_KEYS * PEER_N_KEYS
    row_spec = pl.BlockSpec((tm, D_MODEL), lambda i, j: (i, 0))
    exp_spec = pl.BlockSpec((te, D_MODEL), lambda i, j: (j, 0))
    return pl.pallas_call(
        _peer_mix_body,
        grid=(rows // tm, n_exp // te),
        in_specs=[row_spec, pl.BlockSpec((1, D_MODEL), lambda i, j: (0, 0)),
                  exp_spec, exp_spec,
                  pl.BlockSpec((tm, MIX_FIRST_KEYS, PEER_N_KEYS), lambda i, j: (i, j, 0))],
        out_specs=row_spec,
        out_shape=jax.ShapeDtypeStruct((rows, D_MODEL), F32),
        scratch_shapes=[pltpu.VMEM((tm, D_MODEL), BF16), pltpu.VMEM((tm, D_MODEL), F32)],
        compiler_params=_cparams("parallel", "arbitrary"),
        name="peer_mix",
    )(x, g, u_bf16, v_bf16, wsel)


DEC_REP = 8
DEC_ROWS = N_HEADS_B * DEC_REP
PAGES_PER_STEP = 8


def _head_scores(q_ref, kt_refs):
    return jnp.concatenate(
        [jnp.dot(q_ref[h].astype(BF16),
                 jnp.concatenate([kt[h] for kt in kt_refs], axis=1).astype(BF16),
                 preferred_element_type=F32)
         for h in range(N_HEADS_B)], axis=0)


def _diff_decode_body(pt_ref, q_ref, knew_ref, vnew_ref, sub_ref, lq1_ref, lk1_ref, lq2_ref, lk2_ref,
                      *rest, lam_init):
    kt_refs = rest[:PAGES_PER_STEP]
    v_refs = rest[PAGES_PER_STEP:2 * PAGES_PER_STEP]
    o_ref, m_sc, l_sc, acc_sc = rest[2 * PAGES_PER_STEP:]
    step = pl.program_id(1)

    @pl.when(step == 0)
    def _():
        m_sc[...] = jnp.full(m_sc.shape, NEG_INF, F32)
        l_sc[...] = jnp.zeros(l_sc.shape, F32)
        acc_sc[...] = jnp.zeros(acc_sc.shape, F32)

    s = _head_scores(q_ref, kt_refs) * SCALE
    m_prev = m_sc[...]
    m_new = jnp.maximum(m_prev, jnp.max(s, axis=1, keepdims=True))
    alpha = jnp.exp(m_prev - m_new)
    p = jnp.exp(s - m_new)
    l_sc[...] = alpha * l_sc[...] + jnp.sum(p, axis=1, keepdims=True)
    m_sc[...] = m_new
    p = p.astype(BF16)
    pv = []
    for h in range(N_HEADS_A):
        rows = slice(2 * DEC_REP * h, 2 * DEC_REP * (h + 1))
        v_h = jnp.concatenate(
            [v_ref[pl.ds(h, PAGE_SIZE, stride=N_HEADS_A), :] for v_ref in v_refs], axis=0)
        pv.append(jnp.dot(p[rows], v_h.astype(BF16), preferred_element_type=F32))
    acc_sc[...] = alpha * acc_sc[...] + jnp.concatenate(pv, axis=0)

    @pl.when(step == pl.num_programs(1) - 1)
    def _():
        s_new = jnp.sum(q_ref[...] * knew_ref[...], axis=-1,
                        keepdims=True).reshape(DEC_ROWS, 1) * SCALE
        m_f = jnp.maximum(m_sc[...], s_new)
        a = jnp.exp(m_sc[...] - m_f)
        p_new = jnp.exp(s_new - m_f)
        l_f = a * l_sc[...] + p_new
        o = (a * acc_sc[...] + p_new * vnew_ref[...]) / l_f
        lam = (jnp.exp(jnp.sum(lq1_ref[...] * lk1_ref[...], keepdims=True))
               - jnp.exp(jnp.sum(lq2_ref[...] * lk2_ref[...], keepdims=True)) + lam_init)
        for h in range(N_HEADS_A):
            r0 = 2 * DEC_REP * h
            o_h = o[r0:r0 + DEC_REP] - lam * o[r0 + DEC_REP:r0 + 2 * DEC_REP]
            o_ref[h * DEC_REP:(h + 1) * DEC_REP, :] = (
                _rms(o_h, sub_ref[...], SUBLN_EPS) * (1.0 - lam_init))


def _diff_decode(q, k_new, v_new, cache_kt, cache_v, page_table, subln, lams, lam_init):
    batch, n_pages = page_table.shape
    steps = n_pages // PAGES_PER_STEP
    rep = lambda t: jnp.broadcast_to(t.reshape(batch, N_HEADS_B, 1, HEAD_DIM),
                                     (batch, N_HEADS_B, DEC_REP, HEAD_DIM))
    v_rep = jnp.broadcast_to(v_new.reshape(batch, N_HEADS_A, 1, DV_A),
                             (batch, N_HEADS_A, 2 * DEC_REP, DV_A)).reshape(batch, DEC_ROWS, DV_A)
    q_spec = pl.BlockSpec((None, N_HEADS_B, DEC_REP, HEAD_DIM), lambda b, s, pt: (b, 0, 0, 0))
    vec64 = pl.BlockSpec((1, HEAD_DIM), lambda b, s, pt: (0, 0))

    def page(pg):
        return lambda b, s, pt: (pt[b, s * PAGES_PER_STEP + pg], 0, 0, 0)

    def page3(pg):
        return lambda b, s, pt: (pt[b, s * PAGES_PER_STEP + pg], 0, 0)

    kt_specs = [pl.BlockSpec((None, N_HEADS_B, HEAD_DIM, PAGE_SIZE), page(pg))
                for pg in range(PAGES_PER_STEP)]
    v_specs = [pl.BlockSpec((None, PAGE_SIZE * N_HEADS_A, DV_A), page3(pg))
               for pg in range(PAGES_PER_STEP)]
    out = pl.pallas_call(
        functools.partial(_diff_decode_body, lam_init=lam_init),
        grid_spec=pltpu.PrefetchScalarGridSpec(
            num_scalar_prefetch=1,
            grid=(batch, steps),
            in_specs=[q_spec, q_spec,
                      pl.BlockSpec((None, DEC_ROWS, DV_A), lambda b, s, pt: (b, 0, 0)),
                      pl.BlockSpec((1, DV_A), lambda b, s, pt: (0, 0)),
                      vec64, vec64, vec64, vec64] + kt_specs + v_specs,
            out_specs=pl.BlockSpec((None, N_HEADS_A * DEC_REP, DV_A), lambda b, s, pt: (b, 0, 0)),
            scratch_shapes=[pltpu.VMEM((DEC_ROWS, 1), F32),
                            pltpu.VMEM((DEC_ROWS, 1), F32),
                            pltpu.VMEM((DEC_ROWS, DV_A), F32)]),
        out_shape=jax.ShapeDtypeStruct((batch, N_HEADS_A * DEC_REP, DV_A), F32),
        compiler_params=_cparams("parallel", "arbitrary"),
        name="diff_attn_decode",
    )(page_table, rep(q), rep(k_new), v_rep, subln, *lams,
      *([cache_kt] * PAGES_PER_STEP), *([cache_v] * PAGES_PER_STEP))
    return out[:, ::DEC_REP, :].reshape(batch, D_MODEL)


def _stick_decode_body(pt_ref, q_ref, *rest):
    kt_refs = rest[:PAGES_PER_STEP]
    vt_refs = rest[PAGES_PER_STEP:2 * PAGES_PER_STEP]
    o_ref, c_sc, acc_sc = rest[2 * PAGES_PER_STEP:]
    step = pl.program_id(1)

    @pl.when(step == 0)
    def _():
        c_sc[...] = jnp.zeros(c_sc.shape, F32)
        acc_sc[...] = jnp.zeros(acc_sc.shape, F32)

    tri2 = _neg_tri(PAGE_SIZE, 2)
    carry = c_sc[...]
    z = _head_scores(q_ref, kt_refs) * SCALE
    sp = _softplus(z)
    tails = []
    for pg in range(PAGES_PER_STEP):
        sp_pg = sp[:, pg * PAGE_SIZE:(pg + 1) * PAGE_SIZE]
        tails.append(_later_keys_tail(sp_pg, carry, tri2))
        carry = carry - jnp.sum(sp_pg, axis=1, keepdims=True)
    c_sc[...] = carry
    w = jnp.exp(z - sp + jnp.concatenate(tails, axis=1)).astype(BF16)
    acc_sc[...] = acc_sc[...] + jnp.concatenate(
        [_dot_nt(w[h * DEC_REP:(h + 1) * DEC_REP],
                 jnp.concatenate([vt[h] for vt in vt_refs], axis=1).astype(BF16))
         for h in range(N_HEADS_B)], axis=0)

    @pl.when(step == pl.num_programs(1) - 1)
    def _():
        o_ref[...] = acc_sc[...]


def _stick_decode(q, cache_kt, cache_vt, page_table):
    batch, n_pages = page_table.shape
    steps = n_pages // PAGES_PER_STEP
    q_rep = jnp.broadcast_to(q.reshape(batch, N_HEADS_B, 1, HEAD_DIM),
                             (batch, N_HEADS_B, DEC_REP, HEAD_DIM))

    def page(pg):
        return lambda b, s, pt: (pt[b, n_pages - 1 - (s * PAGES_PER_STEP + pg)], 0, 0, 0)

    specs = [pl.BlockSpec((None, N_HEADS_B, HEAD_DIM, PAGE_SIZE), page(pg))
             for pg in range(PAGES_PER_STEP)]
    out = pl.pallas_call(
        _stick_decode_body,
        grid_spec=pltpu.PrefetchScalarGridSpec(
            num_scalar_prefetch=1,
            grid=(batch, steps),
            in_specs=[pl.BlockSpec((None, N_HEADS_B, DEC_REP, HEAD_DIM),
                                   lambda b, s, pt: (b, 0, 0, 0))] + specs + specs,
            out_specs=pl.BlockSpec((None, DEC_ROWS, HEAD_DIM), lambda b, s, pt: (b, 0, 0)),
            scratch_shapes=[pltpu.VMEM((DEC_ROWS, LANES), F32),
                            pltpu.VMEM((DEC_ROWS, HEAD_DIM), F32)]),
        out_shape=jax.ShapeDtypeStruct((batch, DEC_ROWS, HEAD_DIM), F32),
        compiler_params=_cparams("parallel", "arbitrary"),
        name="stick_attn_decode",
    )(page_table, q_rep, *([cache_kt] * PAGES_PER_STEP), *([cache_vt] * PAGES_PER_STEP))
    return out[:, ::DEC_REP, :].reshape(batch, D_MODEL)


def _keys_last(cache):
    return jnp.transpose(cache, (0, 2, 3, 1))


def kernel(x_prompt, x_sample, p_prompt, p_sample, cache_k_diff, cache_v_diff, cache_k_stick, cache_v_stick, page_table, norm_mix, norm_ffn, norm_ple, norm_final, w_qkv_diff, w_o_diff, lambda_q1, lambda_k1, lambda_q2, lambda_k2, subln_diff, w_qkv_stick, w_o_stick, peer_w_q, peer_sub_keys, peer_u, peer_v, ple_w_gate, ple_w_proj):
    batch, seq, _ = x_prompt.shape
    dec_batch, dec_seq, _ = x_sample.shape
    assert dec_seq == 1
    depth = norm_mix.shape[0]
    n_pages = page_table.shape[1]
    past_len = n_pages * PAGE_SIZE
    n_p, n_s = batch * seq, dec_batch * dec_seq

    xp = x_prompt.reshape(n_p, D_MODEL)
    xs = x_sample.reshape(n_s, D_MODEL)
    cos_p, sin_p = _rope_tables(jnp.arange(seq, dtype=jnp.int32))
    cos_s, sin_s = _rope_tables(jnp.full((n_s,), past_len, jnp.int32))
    vec = lambda t: t.reshape(1, -1)
    g_final = vec(norm_final)

    def rows_major(t, heads):
        return jnp.transpose(t.reshape(batch, heads, HEAD_DIM, seq), (0, 3, 1, 2))

    new_kv = {}
    for i in range(depth):
        j = i // 2
        g_mix, g_ffn, g_ple = vec(norm_mix[i]), vec(norm_ffn[i]), vec(norm_ple[i])
        if i % 2 == 0:
            lam_init = 0.8 - 0.6 * math.exp(-0.3 * i)
            lams = [vec(lambda_q1[j]), vec(lambda_k1[j]), vec(lambda_q2[j]), vec(lambda_k2[j])]
            sub = vec(subln_diff[j])
            w_qkv = w_qkv_diff[j].astype(BF16)
            w_o = w_o_diff[j].astype(BF16)
            qp, kp_t, vp = _qkv(xp, g_mix, w_qkv, cos_p, sin_p, True, seq, k_t=True)
            qs, kn, vn = _qkv(xs, g_mix, w_qkv, cos_s, sin_s, True)
            op = _diff_attn_prompt(qp, kp_t, vp, sub, lams, batch, seq, lam_init)
            kp, vp = rows_major(kp_t, 2 * N_HEADS_A), vp.reshape(batch, seq, N_HEADS_A, DV_A)
            os_ = _diff_decode(qs, kn, vn, _keys_last(cache_k_diff[j]),
                               cache_v_diff[j].reshape(-1, PAGE_SIZE * N_HEADS_A, DV_A),
                               page_table, sub, lams, lam_init)
            heads_k, heads_v = (2 * N_HEADS_A, HEAD_DIM), (N_HEADS_A, DV_A)
            tag = "diff"
        else:
            w_qkv = w_qkv_stick[j].astype(BF16)
            w_o = w_o_stick[j].astype(BF16)
            qp, kp_t, vp_t = _qkv(xp, g_mix, w_qkv, cos_p, sin_p, False, seq, k_t=True, v_t=True)
            qs, kn, vn = _qkv(xs, g_mix, w_qkv, cos_s, sin_s, False)
            op = _stick_attn_prompt(qp, kp_t, vp_t, batch, seq)
            kp, vp = rows_major(kp_t, N_HEADS_B), rows_major(vp_t, N_HEADS_B)
            os_ = _stick_decode(qs, _keys_last(cache_k_stick[j]), _keys_last(cache_v_stick[j]),
                                page_table)
            heads_k = heads_v = (N_HEADS_B, HEAD_DIM)
            tag = "stick"
        new_kv.setdefault(tag, []).append((
            kp, vp,
            kn.reshape(dec_batch, dec_seq, *heads_k), vn.reshape(dec_batch, dec_seq, *heads_v)))
        xp = _proj_residual(op, w_o, xp)
        xs = _proj_residual(os_, w_o, xs)

        wq = peer_w_q[i].astype(BF16)
        sk = peer_sub_keys[i].reshape(2 * PEER_HEADS, PEER_N_KEYS, PEER_HALF).astype(BF16)
        u, v = peer_u[i].astype(BF16), peer_v[i].astype(BF16)
        xp = _peer_mix(xp, g_ffn, u, v, _peer_wsel(*_peer_route(xp, g_ffn, wq, sk)))
        xs = _peer_mix(xs, g_ffn, u, v, _peer_wsel(*_peer_route(xs, g_ffn, wq, sk)))

        wg, wp = ple_w_gate[i].astype(BF16), ple_w_proj[i].astype(BF16)
        last = i == depth - 1
        xp = _ple(xp, p_prompt[i].reshape(n_p, -1), g_ple, wg, wp, g_final, last)
        xs = _ple(xs, p_sample[i].reshape(n_s, -1), g_ple, wg, wp, g_final, last)

    stack = lambda tag, idx: jnp.stack([t[idx] for t in new_kv[tag]])
    return (xp.reshape(batch, seq, D_MODEL), xs.reshape(dec_batch, dec_seq, D_MODEL),
            stack("diff", 0), stack("diff", 1), stack("stick", 0), stack("stick", 1),
            stack("diff", 2), stack("diff", 3), stack("stick", 2), stack("stick", 3))
```
